```python
import math
import jax, jax.numpy as jnp
from jax import lax
import numpy as np

D_MODEL = 2048
BATCH = 4
SEQ = 4096
DEPTH = 1
DEC_BATCH = 32
DEC_SEQ = 1
PAST_LEN = 16384
PAGE_SIZE = 128

N_HEADS = 16
HEAD_DIM = 128
N_KV_HEADS = 4
KV_GROUP = N_HEADS // N_KV_HEADS
ATTN_WIDTH = N_HEADS * HEAD_DIM
MOBA_BLOCK = 256
MOBA_TOPK = 3
Q_CHUNK = 8
SSD_EXPAND = 2
D_INNER = SSD_EXPAND * D_MODEL
SSD_HEAD_DIM = 64
SSD_HEADS = D_INNER // SSD_HEAD_DIM
SSD_GROUPS = 8
SSD_HEADS_PER_GROUP = SSD_HEADS // SSD_GROUPS
D_STATE = 128
CONV_WIDTH = 4
CONV_DIM = D_INNER + 2 * SSD_GROUPS * D_STATE
SSD_CHUNK = 128
D_FF = -(-8 * D_MODEL // (3 * 256)) * 256
EPS = 1e-6

kernel_name = 'moba_ssd_gated_hybrid_step'


def in_splits():
    return (D_MODEL, D_MODEL, ATTN_WIDTH, N_KV_HEADS * HEAD_DIM, N_KV_HEADS * HEAD_DIM, D_INNER, CONV_DIM, SSD_HEADS)


def rmsnorm(x, g):
    xf = x.astype(jnp.float32)
    y = xf * lax.rsqrt(jnp.mean(xf * xf, axis=-1, keepdims=True) + EPS)
    return (y * g.astype(jnp.float32)).astype(x.dtype)


def alibi_slopes():
    return jnp.exp2(-8.0 * jnp.arange(1, N_HEADS + 1, dtype=jnp.float32) / N_HEADS)


def moba_attention(q, k, v, q_start):
    f32 = jnp.float32
    b, sq = q.shape[0], q.shape[1]
    L = k.shape[1]
    nb = -(-L // MOBA_BLOCK)
    pad = nb * MOBA_BLOCK - L
    kb = jnp.pad(k, ((0, 0), (0, pad), (0, 0), (0, 0))).reshape(b, nb, MOBA_BLOCK, N_KV_HEADS, HEAD_DIM)
    vb = jnp.pad(v, ((0, 0), (0, pad), (0, 0), (0, 0))).reshape(b, nb, MOBA_BLOCK, N_KV_HEADS, HEAD_DIM)
    kmean = jnp.mean(kb.astype(f32), axis=2)
    qg = q.reshape(b, sq, N_KV_HEADS, KV_GROUP, HEAD_DIM)
    qpos = q_start + jnp.arange(sq, dtype=jnp.int32)
    qblk = qpos // MOBA_BLOCK
    gate = jnp.einsum('bskgd,bnkd->bskgn', qg.astype(f32), kmean)
    is_past = jnp.arange(nb, dtype=jnp.int32)[None, :] < qblk[:, None]
    gate = jnp.where(is_past[None, :, None, None, :], gate, -jnp.inf)
    n_top = min(MOBA_TOPK, nb)
    _, top_idx = lax.top_k(gate, n_top)
    own = jnp.broadcast_to(qblk[None, :, None, None, None], top_idx.shape[:-1] + (1,))
    blocks = jnp.concatenate([top_idx.astype(jnp.int32), own], axis=-1)
    n_sel = n_top + 1
    sel_valid = jnp.concatenate([jnp.arange(n_top, dtype=jnp.int32)[None, :] < qblk[:, None],
                                 jnp.ones((sq, 1), dtype=bool)], axis=-1)
    kt = kb.transpose(0, 3, 1, 2, 4)
    vt = vb.transpose(0, 3, 1, 2, 4)
    slopes = alibi_slopes().reshape(N_KV_HEADS, KV_GROUP)
    b_ix = jnp.arange(b)[:, None, None, None, None]
    h_ix = jnp.arange(N_KV_HEADS)[None, None, :, None, None]
    r_off = jnp.arange(MOBA_BLOCK, dtype=jnp.int32)
    scale = HEAD_DIM ** -0.5
    qc = math.gcd(sq, Q_CHUNK)
    nc = sq // qc

    def attend(args):
        q_c, blk_c, pos_c, val_c = args
        k_sel = kt[b_ix, h_ix, blk_c]
        v_sel = vt[b_ix, h_ix, blk_c]
        s = jnp.einsum('bckgd,bckgird->bckgir', q_c.astype(f32), k_sel.astype(f32)) * scale
        kpos = blk_c[..., None] * MOBA_BLOCK + r_off
        dist = pos_c[None, :, None, None, None, None] - kpos
        ok = val_c[None, :, None, None, :, None] & (dist >= 0)
        s = jnp.where(ok, s - slopes[None, None, :, :, None, None] * dist.astype(f32), -jnp.inf)
        p = jax.nn.softmax(s.reshape(s.shape[:4] + (n_sel * MOBA_BLOCK,)), axis=-1)
        return jnp.einsum('bckgj,bckgjd->bckgd', p,
                          v_sel.reshape(v_sel.shape[:4] + (n_sel * MOBA_BLOCK, HEAD_DIM)).astype(f32))

    qs = jnp.moveaxis(qg.reshape(b, nc, qc, N_KV_HEADS, KV_GROUP, HEAD_DIM), 1, 0)
    bs = jnp.moveaxis(blocks.reshape(b, nc, qc, N_KV_HEADS, KV_GROUP, n_sel), 1, 0)
    out = lax.map(attend, (qs, bs, qpos.reshape(nc, qc), sel_valid.reshape(nc, qc, n_sel)))
    return jnp.moveaxis(out, 0, 1).reshape(b, sq, ATTN_WIDTH).astype(q.dtype)


def causal_conv(xbc, prev, w, bias):
    t = xbc.shape[1]
    full = jnp.concatenate([prev.astype(xbc.dtype), xbc], axis=1)
    out = bias + full[:, 0:t] * w[0]
    for i in range(1, CONV_WIDTH):
        out = out + full[:, i:i + t] * w[i]
    return jax.nn.silu(out), full[:, -(CONV_WIDTH - 1):]


def ssd_scan(x, dt, A, bm, cm, h0):
    f32 = jnp.float32
    b, t, nh, p = x.shape
    q = math.gcd(t, SSD_CHUNK)
    nc = t // q
    G, E = SSD_GROUPS, SSD_HEADS_PER_GROUP

    def chunks(a, tail):
        return jnp.moveaxis(a.reshape((b, nc, q) + tail), 1, 0)

    xd = chunks(x.astype(f32) * dt[..., None], (G, E, p))
    la = chunks(dt * A, (G, E))
    bc = chunks(bm.astype(f32), (G, D_STATE))
    cc = chunks(cm.astype(f32), (G, D_STATE))
    causal = jnp.tril(jnp.ones((q, q), dtype=bool))

    def step(state, inp):
        xd_c, la_c, b_c, c_c = inp
        acs = jnp.cumsum(la_c, axis=1)
        seg = acs[:, :, None] - acs[:, None, :]
        decay = jnp.exp(jnp.where(causal[None, :, :, None, None], seg, -jnp.inf))
        cb = jnp.einsum('blgn,bsgn->blsg', c_c, b_c)
        y = jnp.einsum('blsg,blsge,bsgep->blgep', cb, decay, xd_c)
        y = y + jnp.einsum('blgn,bgepn->blgep', c_c, state) * jnp.exp(acs)[..., None]
        w_end = jnp.exp(acs[:, -1:] - acs)
        state = state * jnp.exp(acs[:, -1])[..., None, None] + jnp.einsum('bsgn,bsge,bsgep->bgepn', b_c, w_end, xd_c)
        return state, y

    h_fin, ys = lax.scan(step, h0.astype(f32).reshape(b, G, E, p, D_STATE), (xd, la, bc, cc))
    return jnp.moveaxis(ys, 0, 1).reshape(b, t, nh, p), h_fin.reshape(b, nh, p, D_STATE)


def ssd_mixer(z, xbc, dt_raw, conv_prev, h_prev, conv_w, conv_b, dt_bias, a_log, d_skip, ssd_norm):
    f32 = jnp.float32
    b, t = z.shape[0], z.shape[1]
    xbc_c, conv_new = causal_conv(xbc, conv_prev, conv_w, conv_b)
    xs, bm, cm = jnp.split(xbc_c, [D_INNER, D_INNER + SSD_GROUPS * D_STATE], axis=-1)
    xs = xs.reshape(b, t, SSD_HEADS, SSD_HEAD_DIM)
    bm = bm.reshape(b, t, SSD_GROUPS, D_STATE)
    cm = cm.reshape(b, t, SSD_GROUPS, D_STATE)
    dt = jax.nn.softplus(dt_raw.astype(f32) + dt_bias.astype(f32))
    A = -jnp.exp(a_log.astype(f32))
    y, h_new = ssd_scan(xs, dt, A, bm, cm, h_prev)
    y = y + d_skip.astype(f32)[:, None] * xs.astype(f32)
    y = y.reshape(b, t, D_INNER) * jax.nn.silu(z.astype(f32))
    yg = y.reshape(b, t, SSD_GROUPS, D_INNER // SSD_GROUPS)
    yg = yg * lax.rsqrt(jnp.mean(yg * yg, axis=-1, keepdims=True) + EPS)
    y = yg.reshape(b, t, D_INNER) * ssd_norm.astype(f32)
    return y.astype(z.dtype), conv_new, h_new


def decoder_layer(x, k_past, v_past, conv_prev, h_prev, q_start, norm_mix, w_in, conv_w, conv_b, dt_bias,
                  a_log, d_skip, ssd_norm, w_attn_out, w_ssd_out, w_o, norm_ffn, w_ffn_in, w_ffn_out):
    b, t = x.shape[0], x.shape[1]
    hn = rmsnorm(x, norm_mix)
    proj = hn @ w_in
    offs, acc = [], 0
    for size in in_splits()[:-1]:
        acc += size
        offs.append(acc)
    g_attn, g_ssd, q, k, v, z, xbc, dt_raw = jnp.split(proj, offs, axis=-1)
    q = q.reshape(b, t, N_HEADS, HEAD_DIM)
    k = k.reshape(b, t, N_KV_HEADS, HEAD_DIM)
    v = v.reshape(b, t, N_KV_HEADS, HEAD_DIM)
    k_all = k if k_past is None else jnp.concatenate([k_past.astype(k.dtype), k], axis=1)
    v_all = v if v_past is None else jnp.concatenate([v_past.astype(v.dtype), v], axis=1)
    y_attn = moba_attention(q, k_all, v_all, q_start) @ w_attn_out
    y_ssd, conv_new, h_new = ssd_mixer(z, xbc, dt_raw, conv_prev, h_prev, conv_w, conv_b, dt_bias, a_log, d_skip, ssd_norm)
    y_ssd = y_ssd @ w_ssd_out
    mixed = jax.nn.sigmoid(g_attn) * y_attn + jax.nn.sigmoid(g_ssd) * y_ssd
    x = x + mixed @ w_o
    hn = rmsnorm(x, norm_ffn)
    gate, up = jnp.split(hn @ w_ffn_in, [D_FF], axis=-1)
    x = x + (jax.nn.silu(gate) * up) @ w_ffn_out
    return x, k, v, conv_new, h_new


def setup_inputs(seed: int = 0) -> dict:
    key = jax.random.key(seed)
    ks = jax.random.split(key, 24)
    f32 = jnp.float32
    n_pages = PAST_LEN // PAGE_SIZE
    n_phys = (DEC_BATCH * n_pages * 5) // 4
    d_in = sum(in_splits())

    def nrm(k, shape, scale):
        return jax.random.normal(k, shape, f32) * scale

    dt0 = jnp.exp(jax.random.uniform(ks[11], (DEPTH, SSD_HEADS), f32, math.log(1e-3), math.log(1e-1)))
    return {
        'x_prompt': nrm(ks[0], (BATCH, SEQ, D_MODEL), 1.0),
        'x_sample': nrm(ks[1], (DEC_BATCH, DEC_SEQ, D_MODEL), 1.0),
        'cache_k': nrm(ks[2], (DEPTH, n_phys, PAGE_SIZE, N_KV_HEADS, HEAD_DIM), 1.0),
        'cache_v': nrm(ks[3], (DEPTH, n_phys, PAGE_SIZE, N_KV_HEADS, HEAD_DIM), 1.0),
        'state_conv': nrm(ks[4], (DEPTH, DEC_BATCH, CONV_WIDTH - 1, CONV_DIM), 1.0),
        'state_ssm': nrm(ks[5], (DEPTH, DEC_BATCH, SSD_HEADS, SSD_HEAD_DIM, D_STATE), 0.1),
        'page_table': jax.random.permutation(ks[6], n_phys)[:DEC_BATCH * n_pages].reshape(DEC_BATCH, n_pages).astype(jnp.int32),
        'norm_mix': 1.0 + nrm(ks[7], (DEPTH, D_MODEL), 0.01),
        'w_in': nrm(ks[8], (DEPTH, D_MODEL, d_in), D_MODEL ** -0.5),
        'conv_w': nrm(ks[9], (DEPTH, CONV_WIDTH, CONV_DIM), CONV_WIDTH ** -0.5),
        'conv_b': nrm(ks[10], (DEPTH, CONV_DIM), 0.01),
        'dt_bias': dt0 + jnp.log(-jnp.expm1(-dt0)),
        'a_log': jnp.log(jax.random.uniform(ks[12], (DEPTH, SSD_HEADS), f32, 1.0, 16.0)),
        'd_skip': 1.0 + nrm(ks[13], (DEPTH, SSD_HEADS), 0.01),
        'ssd_norm': 1.0 + nrm(ks[14], (DEPTH, D_INNER), 0.01),
        'w_attn_out': nrm(ks[15], (DEPTH, ATTN_WIDTH, D_MODEL), ATTN_WIDTH ** -0.5),
        'w_ssd_out': nrm(ks[16], (DEPTH, D_INNER, D_MODEL), D_INNER ** -0.5),
        'w_o': nrm(ks[17], (DEPTH, D_MODEL, D_MODEL), D_MODEL ** -0.5),
        'norm_ffn': 1.0 + nrm(ks[18], (DEPTH, D_MODEL), 0.01),
        'w_ffn_in': nrm(ks[19], (DEPTH, D_MODEL, 2 * D_FF), D_MODEL ** -0.5),
        'w_ffn_out': nrm(ks[20], (DEPTH, D_FF, D_MODEL), D_FF ** -0.5),
        'norm_final': 1.0 + nrm(ks[21], (D_MODEL,), 0.01),
    }


def reference(x_prompt, x_sample, cache_k, cache_v, state_conv, state_ssm, page_table, norm_mix, w_in, conv_w,
              conv_b, dt_bias, a_log, d_skip, ssd_norm, w_attn_out, w_ssd_out, w_o, norm_ffn, w_ffn_in, w_ffn_out,
              norm_final):
    past_len = page_table.shape[1] * cache_k.shape[2]
    dec_b = x_sample.shape[0]
    bp = x_prompt.shape[0]
    xp, xs = x_prompt, x_sample
    kp_l, vp_l, cp_l, hp_l, ks_l, vs_l, cs_l, hs_l = [], [], [], [], [], [], [], []
    for l in range(DEPTH):
        w = (norm_mix[l], w_in[l], conv_w[l], conv_b[l], dt_bias[l], a_log[l], d_skip[l], ssd_norm[l],
             w_attn_out[l], w_ssd_out[l], w_o[l], norm_ffn[l], w_ffn_in[l], w_ffn_out[l])
        conv0 = jnp.zeros((bp, CONV_WIDTH - 1, CONV_DIM), xp.dtype)
        ssm0 = jnp.zeros((bp, SSD_HEADS, SSD_HEAD_DIM, D_STATE), jnp.float32)
        xp, k_new, v_new, c_new, h_new = decoder_layer(xp, None, None, conv0, ssm0, 0, *w)
        kp_l.append(k_new); vp_l.append(v_new); cp_l.append(c_new); hp_l.append(h_new)
        k_past = cache_k[l][page_table].reshape(dec_b, past_len, N_KV_HEADS, HEAD_DIM)
        v_past = cache_v[l][page_table].reshape(dec_b, past_len, N_KV_HEADS, HEAD_DIM)
        xs, k_new, v_new, c_new, h_new = decoder_layer(xs, k_past, v_past, state_conv[l], state_ssm[l], past_len, *w)
        ks_l.append(k_new); vs_l.append(v_new); cs_l.append(c_new); hs_l.append(h_new)
    y_prompt = rmsnorm(xp, norm_final)
    y_sample = rmsnorm(xs, norm_final)
    return (y_prompt, y_sample, jnp.stack(kp_l), jnp.stack(vp_l), jnp.stack(cp_l), jnp.stack(hp_l),
            jnp.stack(ks_l), jnp.stack(vs_l), jnp.stack(cs_l), jnp.stack(hs_l))
```

```python
import functools

import jax
import jax.numpy as jnp
from jax import lax
from jax.experimental import pallas as pl
from jax.experimental.pallas import tpu as pltpu

F32 = jnp.float32
BF16 = jnp.bfloat16

D_MODEL = 2048
N_HEADS = 16
HEAD_DIM = 128
N_KV_HEADS = 4
KV_GROUP = N_HEADS // N_KV_HEADS
ATTN_WIDTH = N_HEADS * HEAD_DIM
KV_WIDTH = N_KV_HEADS * HEAD_DIM
MOBA_BLOCK = 256
MOBA_TOPK = 3
D_INNER = 4096
SSD_HEAD_DIM = 64
SSD_HEADS = D_INNER // SSD_HEAD_DIM
SSD_GROUPS = 8
GROUP_WIDTH = D_INNER // SSD_GROUPS
D_STATE = 128
CONV_WIDTH = 4
CONV_DIM = D_INNER + 2 * SSD_GROUPS * D_STATE
SSD_CHUNK = 128
D_FF = 5632
EPS = 1e-6
LANES = 128
NEG_BIG = -1e30

VMEM_LIMIT = 48 * 1024 * 1024


def _params(*sem):
    return pltpu.CompilerParams(dimension_semantics=sem, vmem_limit_bytes=VMEM_LIMIT)


def _split_hi_lo(x):
    hi = x.astype(BF16)
    lo = (x - hi.astype(F32)).astype(BF16)
    return hi, lo


def _silu(x):
    return x / (1.0 + jnp.exp(-x))


def _softplus(x):
    return jnp.maximum(x, 0.0) + jnp.log1p(jnp.exp(-jnp.abs(x)))


def _rmsnorm_kernel(x_ref, g_ref, o_ref):
    x = x_ref[...]
    y = x * lax.rsqrt(jnp.mean(x * x, axis=-1, keepdims=True) + EPS)
    o_ref[...] = (y * g_ref[...]).astype(o_ref.dtype)


def _rmsnorm(x, g, out_dtype=BF16, tm=512):
    m, d = x.shape
    tm = min(tm, m)
    return pl.pallas_call(
        _rmsnorm_kernel,
        grid=(pl.cdiv(m, tm),),
        in_specs=[pl.BlockSpec((tm, d), lambda i: (i, 0)), pl.BlockSpec((1, d), lambda i: (0, 0))],
        out_specs=pl.BlockSpec((tm, d), lambda i: (i, 0)),
        out_shape=jax.ShapeDtypeStruct((m, d), out_dtype),
        compiler_params=_params("parallel"),
        name="rmsnorm",
    )(x, g.reshape(1, d))


def _matmul_kernel(a_ref, w_ref, o_ref):
    o_ref[...] = jnp.dot(a_ref[...], w_ref[...], preferred_element_type=F32).astype(o_ref.dtype)


def _matmul(a, w, out_dtype=F32, tm=1024, tn=512, name="matmul"):
    m, k = a.shape
    n = w.shape[1]
    tm, tn = min(tm, m), min(tn, n)
    return pl.pallas_call(
        _matmul_kernel,
        grid=(pl.cdiv(m, tm), n // tn),
        in_specs=[pl.BlockSpec((tm, k), lambda i, j: (i, 0)), pl.BlockSpec((k, tn), lambda i, j: (0, j))],
        out_specs=pl.BlockSpec((tm, tn), lambda i, j: (i, j)),
        out_shape=jax.ShapeDtypeStruct((m, n), out_dtype),
        compiler_params=_params("parallel", "arbitrary"),
        name=name,
    )(a, w)


def _mix_kernel(att_ref, ssd_ref, wa_ref, ws_ref, ga_ref, gs_ref, o_ref):
    ya = jnp.dot(att_ref[...], wa_ref[...], preferred_element_type=F32)
    ys = jnp.dot(ssd_ref[...], ws_ref[...], preferred_element_type=F32)
    o_ref[...] = (jax.nn.sigmoid(ga_ref[...]) * ya + jax.nn.sigmoid(gs_ref[...]) * ys).astype(o_ref.dtype)


def _mix(att, ssd, wa, ws, gates, tm=1024, tn=256):
    m = att.shape[0]
    tm = min(tm, m)
    nj = D_MODEL // tn
    return pl.pallas_call(
        _mix_kernel,
        grid=(pl.cdiv(m, tm), nj),
        in_specs=[
            pl.BlockSpec((tm, ATTN_WIDTH), lambda i, j: (i, 0)),
            pl.BlockSpec((tm, D_INNER), lambda i, j: (i, 0)),
            pl.BlockSpec((ATTN_WIDTH, tn), lambda i, j: (0, j)),
            pl.BlockSpec((D_INNER, tn), lambda i, j: (0, j)),
            pl.BlockSpec((tm, tn), lambda i, j: (i, j)),
            pl.BlockSpec((tm, tn), lambda i, j: (i, j + nj)),
        ],
        out_specs=pl.BlockSpec((tm, tn), lambda i, j: (i, j)),
        out_shape=jax.ShapeDtypeStruct((m, D_MODEL), BF16),
        compiler_params=_params("parallel", "arbitrary"),
        name="mix",
    )(att, ssd, wa, ws, gates, gates)


def _oproj_kernel(mix_ref, wo_ref, x_ref, g_ref, x1_ref, hn_ref):
    x1 = x_ref[...] + jnp.dot(mix_ref[...], wo_ref[...], preferred_element_type=F32)
    x1_ref[...] = x1
    y = x1 * lax.rsqrt(jnp.mean(x1 * x1, axis=-1, keepdims=True) + EPS)
    hn_ref[...] = (y * g_ref[...]).astype(hn_ref.dtype)


def _oproj(mixed, wo, x, g, tm=512):
    m = x.shape[0]
    tm = min(tm, m)
    return pl.pallas_call(
        _oproj_kernel,
        grid=(pl.cdiv(m, tm),),
        in_specs=[
            pl.BlockSpec((tm, D_MODEL), lambda i: (i, 0)),
            pl.BlockSpec((D_MODEL, D_MODEL), lambda i: (0, 0)),
            pl.BlockSpec((tm, D_MODEL), lambda i: (i, 0)),
            pl.BlockSpec((1, D_MODEL), lambda i: (0, 0)),
        ],
        out_specs=[pl.BlockSpec((tm, D_MODEL), lambda i: (i, 0)), pl.BlockSpec((tm, D_MODEL), lambda i: (i, 0))],
        out_shape=[jax.ShapeDtypeStruct((m, D_MODEL), F32), jax.ShapeDtypeStruct((m, D_MODEL), BF16)],
        compiler_params=_params("parallel"),
        name="oproj",
    )(mixed, wo, x, g.reshape(1, D_MODEL))


def _ffn_in_kernel(h_ref, wg_ref, wu_ref, o_ref):
    h = h_ref[...]
    gate = jnp.dot(h, wg_ref[...], preferred_element_type=F32)
    up = jnp.dot(h, wu_ref[...], preferred_element_type=F32)
    o_ref[...] = (_silu(gate) * up).astype(o_ref.dtype)


def _ffn_in(hn, w, tm=1024, tn=512):
    m = hn.shape[0]
    tm = min(tm, m)
    nj = D_FF // tn
    return pl.pallas_call(
        _ffn_in_kernel,
        grid=(pl.cdiv(m, tm), nj),
        in_specs=[
            pl.BlockSpec((tm, D_MODEL), lambda i, j: (i, 0)),
            pl.BlockSpec((D_MODEL, tn), lambda i, j: (0, j)),
            pl.BlockSpec((D_MODEL, tn), lambda i, j: (0, j + nj)),
        ],
        out_specs=pl.BlockSpec((tm, tn), lambda i, j: (i, j)),
        out_shape=jax.ShapeDtypeStruct((m, D_FF), BF16),
        compiler_params=_params("parallel", "arbitrary"),
        name="ffn_in",
    )(hn, w, w)


def _ffn_out_kernel(h_ref, w_ref, x_ref, g_ref, o_ref):
    k = pl.program_id(1)

    @pl.when(k == 0)
    def _():
        o_ref[...] = x_ref[...]

    o_ref[...] += jnp.dot(h_ref[...], w_ref[...], preferred_element_type=F32)

    @pl.when(k == pl.num_programs(1) - 1)
    def _():
        x2 = o_ref[...]
        y = x2 * lax.rsqrt(jnp.mean(x2 * x2, axis=-1, keepdims=True) + EPS)
        o_ref[...] = y * g_ref[...]


def _ffn_out(h, w, x1, g, tm=512, tk=512):
    m = h.shape[0]
    tm = min(tm, m)
    return pl.pallas_call(
        _ffn_out_kernel,
        grid=(pl.cdiv(m, tm), D_FF // tk),
        in_specs=[
            pl.BlockSpec((tm, tk), lambda i, k: (i, k)),
            pl.BlockSpec((tk, D_MODEL), lambda i, k: (k, 0)),
            pl.BlockSpec((tm, D_MODEL), lambda i, k: (i, 0)),
            pl.BlockSpec((1, D_MODEL), lambda i, k: (0, 0)),
        ],
        out_specs=pl.BlockSpec((tm, D_MODEL), lambda i, k: (i, 0)),
        out_shape=jax.ShapeDtypeStruct((m, D_MODEL), F32),
        compiler_params=_params("parallel", "arbitrary"),
        name="ffn_out",
    )(h, w, x1, g.reshape(1, D_MODEL))


def _kmean_kernel(k_ref, o_ref):
    t = k_ref.shape[0]
    nb = t // MOBA_BLOCK
    km = jnp.mean(k_ref[...].reshape(nb, MOBA_BLOCK, KV_WIDTH), axis=1)
    o_ref[0] = jnp.concatenate([km, jnp.zeros((LANES - nb, KV_WIDTH), F32)], axis=0)


def _kmean(k, batch):
    t = k.shape[0] // batch
    return pl.pallas_call(
        _kmean_kernel,
        grid=(batch,),
        in_specs=[pl.BlockSpec((t, KV_WIDTH), lambda b: (b, 0))],
        out_specs=pl.BlockSpec((1, LANES, KV_WIDTH), lambda b: (b, 0, 0)),
        out_shape=jax.ShapeDtypeStruct((batch, LANES, KV_WIDTH), F32),
        compiler_params=_params("parallel"),
        name="kmean",
    )(k)


def _first_argmax(g, lane):
    m = jnp.max(g, axis=-1, keepdims=True)
    return jnp.min(jnp.where(g == m, lane, LANES), axis=-1, keepdims=True)


def _moba_kernel(slopes_ref, q_ref, k_ref, v_ref, km_ref, o_ref, kb_ref, vb_ref, m_ref, l_ref, acc_ref):
    kvh = pl.program_id(1)
    i = pl.program_id(2)
    rows = KV_GROUP * MOBA_BLOCK

    @pl.when(i == 0)
    def _():
        kb_ref[...] = k_ref[...].astype(BF16)
        vb_ref[...] = v_ref[...].astype(BF16)

    q = q_ref[...]
    qs = jnp.concatenate([q[:, g * HEAD_DIM:(g + 1) * HEAD_DIM] for g in range(KV_GROUP)], axis=0)
    qsb = (qs * (HEAD_DIM ** -0.5)).astype(BF16)

    contract_last = (((1,), (1,)), ((), ()))
    q_hi, q_lo = _split_hi_lo(qs)
    km_hi, km_lo = _split_hi_lo(km_ref[0])
    gate = (lax.dot_general(q_hi, km_hi, contract_last, preferred_element_type=F32)
            + lax.dot_general(q_lo, km_hi, contract_last, preferred_element_type=F32)
            + lax.dot_general(q_hi, km_lo, contract_last, preferred_element_type=F32))
    lane = lax.broadcasted_iota(jnp.int32, (rows, LANES), 1)
    gate = jnp.where(lane < i, gate, -jnp.inf)
    picks = []
    for t in range(MOBA_TOPK):
        idx = _first_argmax(gate, lane)
        picks.append(jnp.where(t < i, idx, -1))
        gate = jnp.where(lane == idx, -jnp.inf, gate)

    row = lax.broadcasted_iota(jnp.int32, (rows, 1), 0)
    slope = jnp.zeros((rows, 1), F32)
    for g in range(KV_GROUP):
        slope = jnp.where(row // MOBA_BLOCK == g, slopes_ref[kvh * KV_GROUP + g], slope)
    kpos = lax.broadcasted_iota(jnp.int32, (1, MOBA_BLOCK), 1)

    m_ref[...] = jnp.full((rows, 1), NEG_BIG, F32)
    l_ref[...] = jnp.zeros((rows, 1), F32)
    acc_ref[...] = jnp.zeros((rows, HEAD_DIM), F32)

    def past_block(n, carry):
        start = pl.multiple_of(n * MOBA_BLOCK, MOBA_BLOCK)
        kb = kb_ref[pl.ds(start, MOBA_BLOCK), :]
        vb = vb_ref[pl.ds(start, MOBA_BLOCK), :]
        s = lax.dot_general(qsb, kb, contract_last, preferred_element_type=F32)
        s = s + slope * (kpos - (i - n) * MOBA_BLOCK).astype(F32)
        sel = (picks[0] == n) | (picks[1] == n) | (picks[2] == n)
        m_old = m_ref[...]
        m_new = jnp.where(sel, jnp.maximum(m_old, jnp.max(s, axis=-1, keepdims=True)), m_old)
        p = jnp.exp(s - jnp.where(sel, m_new, -NEG_BIG))
        alpha = jnp.exp(m_old - m_new)
        l_ref[...] = alpha * l_ref[...] + jnp.sum(p, axis=-1, keepdims=True)
        acc_ref[...] = alpha * acc_ref[...] + jnp.dot(p.astype(BF16), vb, preferred_element_type=F32)
        m_ref[...] = m_new
        return carry

    lax.fori_loop(0, i, past_block, 0)

    start = pl.multiple_of(i * MOBA_BLOCK, MOBA_BLOCK)
    kb = kb_ref[pl.ds(start, MOBA_BLOCK), :]
    vb = vb_ref[pl.ds(start, MOBA_BLOCK), :]
    s = lax.dot_general(qsb, kb, contract_last, preferred_element_type=F32) + slope * kpos.astype(F32)
    s = jnp.where(kpos <= row % MOBA_BLOCK, s, NEG_BIG)
    m_old = m_ref[...]
    m_new = jnp.maximum(m_old, jnp.max(s, axis=-1, keepdims=True))
    p = jnp.exp(s - m_new)
    alpha = jnp.exp(m_old - m_new)
    l = alpha * l_ref[...] + jnp.sum(p, axis=-1, keepdims=True)
    acc = alpha * acc_ref[...] + jnp.dot(p.astype(BF16), vb, preferred_element_type=F32)
    out = acc / l
    o_ref[...] = jnp.concatenate(
        [out[g * MOBA_BLOCK:(g + 1) * MOBA_BLOCK] for g in range(KV_GROUP)], axis=1).astype(o_ref.dtype)


def _moba_prompt(q, k, v, kmean, slopes, batch):
    t = q.shape[0] // batch
    nb = t // MOBA_BLOCK
    rows = KV_GROUP * MOBA_BLOCK
    grid_spec = pltpu.PrefetchScalarGridSpec(
        num_scalar_prefetch=1,
        grid=(batch, N_KV_HEADS, nb),
        in_specs=[
            pl.BlockSpec((MOBA_BLOCK, KV_GROUP * HEAD_DIM), lambda b, h, i, s: (b * nb + i, h)),
            pl.BlockSpec((t, HEAD_DIM), lambda b, h, i, s: (b, h)),
            pl.BlockSpec((t, HEAD_DIM), lambda b, h, i, s: (b, h)),
            pl.BlockSpec((1, LANES, HEAD_DIM), lambda b, h, i, s: (b, 0, h)),
        ],
        out_specs=pl.BlockSpec((MOBA_BLOCK, KV_GROUP * HEAD_DIM), lambda b, h, i, s: (b * nb + i, h)),
        scratch_shapes=[
            pltpu.VMEM((t, HEAD_DIM), BF16),
            pltpu.VMEM((t, HEAD_DIM), BF16),
            pltpu.VMEM((rows, 1), F32),
            pltpu.VMEM((rows, 1), F32),
            pltpu.VMEM((rows, HEAD_DIM), F32),
        ],
    )
    return pl.pallas_call(
        _moba_kernel,
        grid_spec=grid_spec,
        out_shape=jax.ShapeDtypeStruct((batch * t, ATTN_WIDTH), BF16),
        compiler_params=_params("arbitrary", "arbitrary", "arbitrary"),
        name="moba_prompt",
    )(slopes, q, k, v, kmean)


def _ssd_prompt_kernel(xbc_ref, z_ref, dt_ref, cw_ref, cb_ref, dtb_ref, alog_ref, dsk_ref, nrm_ref, e_ref,
                       y_ref, st_ref, xpad_ref):
    c = pl.program_id(1)
    L = SSD_CHUNK
    head = 8

    @pl.when(c == 0)
    def _():
        st_ref[...] = jnp.zeros_like(st_ref)
        xpad_ref[0:head, :] = jnp.zeros((head, CONV_DIM), F32)

    xpad_ref[head:head + L, :] = xbc_ref[...]
    off = head - (CONV_WIDTH - 1)
    conv = cb_ref[...] + xpad_ref[off:off + L, :] * cw_ref[0:1, :]
    for i in range(1, CONV_WIDTH):
        conv = conv + xpad_ref[off + i:off + i + L, :] * cw_ref[i:i + 1, :]
    xpad_ref[0:head, :] = xpad_ref[L:L + head, :]
    act = _silu(conv)
    xs = act[:, :D_INNER]
    bm = act[:, D_INNER:D_INNER + SSD_GROUPS * D_STATE]
    cm = act[:, D_INNER + SSD_GROUPS * D_STATE:]

    dt = _softplus(dt_ref[...] + dtb_ref[...])
    la = dt * (-jnp.exp(alog_ref[...]))
    r_i = lax.broadcasted_iota(jnp.int32, (L, L), 0)
    c_i = lax.broadcasted_iota(jnp.int32, (L, L), 1)
    causal = r_i >= c_i
    tri = causal.astype(BF16)
    la_hi = la.astype(BF16)
    la_r = la - la_hi.astype(F32)
    la_mid = la_r.astype(BF16)
    la_lo = (la_r - la_mid.astype(F32)).astype(BF16)
    acs = (jnp.dot(tri, la_hi, preferred_element_type=F32) + jnp.dot(tri, la_mid, preferred_element_type=F32)
           + jnp.dot(tri, la_lo, preferred_element_type=F32))
    acs_t = acs.T
    dt_t = dt.T
    acs_last = acs[L - 1:L, :]
    e = e_ref[...]
    exp_acs_x = jnp.dot(jnp.exp(acs).astype(BF16), e, preferred_element_type=F32)
    wend_x = jnp.dot((dt * jnp.exp(acs_last - acs)).astype(BF16), e, preferred_element_type=F32)
    dec_last = jnp.exp(acs_t[:, L - 1:L])
    lane = lax.broadcasted_iota(jnp.int32, (L, LANES), 1)
    contract_last = (((1,), (1,)), ((), ()))
    contract_first = (((0,), (0,)), ((), ()))
    z = z_ref[...]

    for g in range(SSD_GROUPS):
        gs = slice(g * GROUP_WIDTH, (g + 1) * GROUP_WIDTH)
        bg = bm[:, g * D_STATE:(g + 1) * D_STATE].astype(BF16)
        cg = cm[:, g * D_STATE:(g + 1) * D_STATE].astype(BF16)
        cb = lax.dot_general(cg, bg, contract_last, preferred_element_type=F32)
        st = st_ref[0, gs, :]
        y_state = lax.dot_general(cg, st.astype(BF16), contract_last, preferred_element_type=F32)
        pairs = []
        for j in range(GROUP_WIDTH // LANES):
            col = g * (GROUP_WIDTH // LANES) + j
            ms = []
            for h in (2 * col, 2 * col + 1):
                seg = acs[:, h:h + 1] - acs_t[h:h + 1, :]
                ms.append(cb * jnp.exp(jnp.where(causal, seg, -jnp.inf)) * dt_t[h:h + 1, :])
            xcol = xs[:, col * LANES:(col + 1) * LANES]
            rhs = jnp.concatenate([jnp.where(lane < SSD_HEAD_DIM, xcol, 0.0),
                                   jnp.where(lane >= SSD_HEAD_DIM, xcol, 0.0)], axis=0).astype(BF16)
            lhs = jnp.concatenate(ms, axis=1).astype(BF16)
            pairs.append(jnp.dot(lhs, rhs, preferred_element_type=F32))
        xg = xs[:, gs]
        yg = jnp.concatenate(pairs, axis=1) + y_state * exp_acs_x[:, gs] + dsk_ref[:, gs] * xg
        yg = yg * _silu(z[:, gs])
        yg = yg * lax.rsqrt(jnp.mean(yg * yg, axis=-1, keepdims=True) + EPS)
        y_ref[:, gs] = (yg * nrm_ref[:, gs]).astype(y_ref.dtype)

        upd = lax.dot_general((xg * wend_x[:, gs]).astype(BF16), bg, contract_first, preferred_element_type=F32)
        drows = jnp.concatenate(
            [jnp.broadcast_to(dec_last[h:h + 1, :], (SSD_HEAD_DIM, D_STATE))
             for h in range(g * 8, (g + 1) * 8)], axis=0)
        st_ref[0, gs, :] = st * drows + upd


def _ssd_prompt(xbc, z, dt_raw, cw, cb, dtb, alog, dsk_x, nrm, e, batch):
    t = xbc.shape[0] // batch
    nc = t // SSD_CHUNK
    row = lambda b, c: (b * nc + c, 0)
    const = lambda b, c: (0, 0)
    return pl.pallas_call(
        _ssd_prompt_kernel,
        grid=(batch, nc),
        in_specs=[
            pl.BlockSpec((SSD_CHUNK, CONV_DIM), row),
            pl.BlockSpec((SSD_CHUNK, D_INNER), row),
            pl.BlockSpec((SSD_CHUNK, LANES), row),
            pl.BlockSpec((CONV_WIDTH, CONV_DIM), const),
            pl.BlockSpec((1, CONV_DIM), const),
            pl.BlockSpec((1, LANES), const),
            pl.BlockSpec((1, LANES), const),
            pl.BlockSpec((1, D_INNER), const),
            pl.BlockSpec((1, D_INNER), const),
            pl.BlockSpec((LANES, D_INNER), const),
        ],
        out_specs=[
            pl.BlockSpec((SSD_CHUNK, D_INNER), row),
            pl.BlockSpec((1, D_INNER, D_STATE), lambda b, c: (b, 0, 0)),
        ],
        out_shape=[
            jax.ShapeDtypeStruct((batch * t, D_INNER), BF16),
            jax.ShapeDtypeStruct((batch, D_INNER, D_STATE), F32),
        ],
        scratch_shapes=[pltpu.VMEM((SSD_CHUNK + 8, CONV_DIM), F32)],
        compiler_params=_params("arbitrary", "arbitrary"),
        name="ssd_prompt",
    )(xbc, z, dt_raw, cw, cb, dtb, alog, dsk_x, nrm, e)


def _dec_attn_kernel(pt_ref, slopes_ref, q_ref, k0_ref, k1_ref, v0_ref, v1_ref,
                     o_ref, m_ref, l_ref, g_ref, *, past_len):
    n = pl.program_id(1)
    q = q_ref[0]
    hrow = lax.broadcasted_iota(jnp.int32, (N_HEADS, 1), 0)
    qbd = jnp.concatenate([jnp.where(hrow // KV_GROUP == c, q, 0.0) for c in range(N_KV_HEADS)], axis=1)
    kblk = jnp.concatenate([k0_ref[0], k1_ref[0]], axis=0)
    vblk = jnp.concatenate([v0_ref[0], v1_ref[0]], axis=0)
    contract_last = (((1,), (1,)), ((), ()))
    s = lax.dot_general(qbd.astype(BF16), kblk.astype(BF16), contract_last, preferred_element_type=F32)
    slope = jnp.zeros((N_HEADS, 1), F32)
    for h in range(N_HEADS):
        slope = jnp.where(hrow == h, slopes_ref[h], slope)
    kpos = n * MOBA_BLOCK + lax.broadcasted_iota(jnp.int32, (1, MOBA_BLOCK), 1)
    s = s * (HEAD_DIM ** -0.5) - slope * (past_len - kpos).astype(F32)
    m = jnp.max(s, axis=-1, keepdims=True)
    p = jnp.exp(s - m)
    l = jnp.sum(p, axis=-1, keepdims=True)
    o = jnp.dot(p.astype(BF16), vblk.astype(BF16), preferred_element_type=F32)
    o_ref[0, 0] = sum(jnp.where(hrow // KV_GROUP == c, o[:, c * HEAD_DIM:(c + 1) * HEAD_DIM], 0.0)
                      for c in range(N_KV_HEADS))
    kmean = jnp.mean(kblk, axis=0, keepdims=True)
    gate = jnp.sum(qbd * kmean, axis=-1, keepdims=True)

    @pl.when(n == 0)
    def _():
        m_ref[...] = jnp.zeros_like(m_ref)
        l_ref[...] = jnp.zeros_like(l_ref)
        g_ref[...] = jnp.zeros_like(g_ref)

    lane = lax.broadcasted_iota(jnp.int32, (N_HEADS, LANES), 1)
    m_ref[0] = jnp.where(lane == n, m, m_ref[0])
    l_ref[0] = jnp.where(lane == n, l, l_ref[0])
    g_ref[0] = jnp.where(lane == n, gate, g_ref[0])


def _dec_attn_partials(q, cache_k, cache_v, page_table, slopes):
    nseq, n_pages = page_table.shape
    page = cache_k.shape[1]
    assert MOBA_BLOCK == 2 * page
    nb = n_pages * page // MOBA_BLOCK
    assert nb <= LANES
    kc = cache_k.reshape(cache_k.shape[0], page, KV_WIDTH)
    vc = cache_v.reshape(cache_v.shape[0], page, KV_WIDTH)
    page_spec = lambda which: pl.BlockSpec((1, page, KV_WIDTH), lambda b, n, pt, s: (pt[b, 2 * n + which], 0, 0))
    stat_spec = pl.BlockSpec((1, N_HEADS, LANES), lambda b, n, pt, s: (b, 0, 0))
    grid_spec = pltpu.PrefetchScalarGridSpec(
        num_scalar_prefetch=2,
        grid=(nseq, nb),
        in_specs=[pl.BlockSpec((1, N_HEADS, HEAD_DIM), lambda b, n, pt, s: (b, 0, 0)),
                  page_spec(0), page_spec(1), page_spec(0), page_spec(1)],
        out_specs=[pl.BlockSpec((1, 1, N_HEADS, HEAD_DIM), lambda b, n, pt, s: (b, n, 0, 0)),
                   stat_spec, stat_spec, stat_spec],
    )
    stat = jax.ShapeDtypeStruct((nseq, N_HEADS, LANES), F32)
    return pl.pallas_call(
        functools.partial(_dec_attn_kernel, past_len=n_pages * page),
        grid_spec=grid_spec,
        out_shape=[jax.ShapeDtypeStruct((nseq, nb, N_HEADS, HEAD_DIM), F32), stat, stat, stat],
        compiler_params=_params("arbitrary", "arbitrary"),
        name="dec_attn_partials",
    )(page_table, slopes, q.reshape(nseq, N_HEADS, HEAD_DIM), kc, kc, vc, vc)


def _dec_combine_kernel(q_ref, kn_ref, vn_ref, o_ref, m_ref, l_ref, g_ref, out_ref, *, nb):
    lane = lax.broadcasted_iota(jnp.int32, (N_HEADS, LANES), 1)
    hrow = lax.broadcasted_iota(jnp.int32, (N_HEADS, 1), 0)
    gate = jnp.where(lane < nb, g_ref[0], -jnp.inf)
    sel = jnp.zeros((N_HEADS, LANES), jnp.bool_)
    for _ in range(min(MOBA_TOPK, nb)):
        idx = _first_argmax(gate, lane)
        sel = sel | (lane == idx)
        gate = jnp.where(lane == idx, -jnp.inf, gate)
    kx = jnp.zeros((N_HEADS, HEAD_DIM), F32)
    vx = jnp.zeros((N_HEADS, HEAD_DIM), F32)
    for c in range(N_KV_HEADS):
        kx = jnp.where(hrow // KV_GROUP == c, kn_ref[0, c:c + 1, :], kx)
        vx = jnp.where(hrow // KV_GROUP == c, vn_ref[0, c:c + 1, :], vx)
    s_self = jnp.sum(q_ref[0] * kx, axis=-1, keepdims=True) * (HEAD_DIM ** -0.5)
    mstar = jnp.maximum(jnp.max(jnp.where(sel, m_ref[0], -jnp.inf), axis=-1, keepdims=True), s_self)
    w = jnp.where(sel, jnp.exp(m_ref[0] - mstar), 0.0)
    w_self = jnp.exp(s_self - mstar)
    denom = jnp.sum(w * l_ref[0], axis=-1, keepdims=True) + w_self
    num = w_self * vx
    for n in range(nb):
        num = num + w[:, n:n + 1] * o_ref[0, n]
    out_ref[0] = (num / denom).astype(out_ref.dtype)


def _dec_combine(q, k_new, v_new, o_part, m_part, l_part, g_part):
    nseq, nb = o_part.shape[0], o_part.shape[1]
    head_spec = pl.BlockSpec((1, N_HEADS, HEAD_DIM), lambda b: (b, 0, 0))
    kv_spec = pl.BlockSpec((1, N_KV_HEADS, HEAD_DIM), lambda b: (b, 0, 0))
    stat_spec = pl.BlockSpec((1, N_HEADS, LANES), lambda b: (b, 0, 0))
    out = pl.pallas_call(
        functools.partial(_dec_combine_kernel, nb=nb),
        grid=(nseq,),
        in_specs=[head_spec, kv_spec, kv_spec,
                  pl.BlockSpec((1, nb, N_HEADS, HEAD_DIM), lambda b: (b, 0, 0, 0)),
                  stat_spec, stat_spec, stat_spec],
        out_specs=head_spec,
        out_shape=jax.ShapeDtypeStruct((nseq, N_HEADS, HEAD_DIM), BF16),
        compiler_params=_params("parallel"),
        name="dec_attn_combine",
    )(q.reshape(nseq, N_HEADS, HEAD_DIM), k_new.reshape(nseq, N_KV_HEADS, HEAD_DIM),
      v_new.reshape(nseq, N_KV_HEADS, HEAD_DIM), o_part, m_part, l_part, g_part)
    return out.reshape(nseq, ATTN_WIDTH)


def _ssd_step_prep_kernel(xbc_ref, cs_ref, dt_ref, cw_ref, cb_ref, dtb_ref, alog_ref, e_ref,
                          xs_ref, b_ref, c_ref, xd_ref, da_ref, cn_ref):
    x = xbc_ref[...]
    conv = cb_ref[...] + cs_ref[0] * cw_ref[0:1, :]
    for i in range(1, CONV_WIDTH - 1):
        conv = conv + cs_ref[i] * cw_ref[i:i + 1, :]
    conv = conv + x * cw_ref[CONV_WIDTH - 1:CONV_WIDTH, :]
    for i in range(CONV_WIDTH - 2):
        cn_ref[i] = cs_ref[i + 1]
    cn_ref[CONV_WIDTH - 2] = x
    act = _silu(conv)
    xs = act[:, :D_INNER]
    xs_ref[...] = xs
    b_ref[...] = act[:, D_INNER:D_INNER + SSD_GROUPS * D_STATE]
    c_ref[...] = act[:, D_INNER + SSD_GROUPS * D_STATE:]
    dt = _softplus(dt_ref[...] + dtb_ref[...])
    da = jnp.exp(dt * (-jnp.exp(alog_ref[...])))
    e = e_ref[...]
    dt_hi, dt_lo = _split_hi_lo(dt)
    da_hi, da_lo = _split_hi_lo(da)
    dt_x = jnp.dot(dt_hi, e, preferred_element_type=F32) + jnp.dot(dt_lo, e, preferred_element_type=F32)
    da_ref[...] = jnp.dot(da_hi, e, preferred_element_type=F32) + jnp.dot(da_lo, e, preferred_element_type=F32)
    xd_ref[...] = xs * dt_x


def _ssd_step_prep(xbc, conv_state_t, dt_raw, cw, cb, dtb, alog, e):
    n = xbc.shape[0]
    sds = lambda *shape: jax.ShapeDtypeStruct(shape, F32)
    return pl.pallas_call(
        _ssd_step_prep_kernel,
        out_shape=[sds(n, D_INNER), sds(n, SSD_GROUPS * D_STATE), sds(n, SSD_GROUPS * D_STATE),
                   sds(n, D_INNER), sds(n, D_INNER), sds(CONV_WIDTH - 1, n, CONV_DIM)],
        compiler_params=pltpu.CompilerParams(vmem_limit_bytes=VMEM_LIMIT),
        name="ssd_step_prep",
    )(xbc, conv_state_t, dt_raw, cw, cb, dtb, alog, e)


def _ssd_step_kernel(xd_ref, da_ref, xs_ref, z_ref, b_ref, c_ref, st_ref, dsk_ref, nrm_ref, y_ref, sn_ref):
    sub = 8
    row0 = lax.broadcasted_iota(jnp.int32, (sub, D_STATE), 0) == 0
    contract_last = (((1,), (1,)), ((), ()))
    contract_first = (((0,), (0,)), ((), ()))
    ones0 = jnp.where(row0, 1.0, 0.0).astype(BF16)
    for g in range(SSD_GROUPS):
        gs = slice(g * GROUP_WIDTH, (g + 1) * GROUP_WIDTH)
        ns = slice(g * D_STATE, (g + 1) * D_STATE)
        xd8 = jnp.broadcast_to(xd_ref[0, :, gs], (sub, GROUP_WIDTH))
        da8 = jnp.broadcast_to(da_ref[0, :, gs], (sub, GROUP_WIDTH))
        b8 = jnp.where(row0, jnp.broadcast_to(b_ref[0, :, ns], (sub, D_STATE)), 0.0)
        c8 = jnp.broadcast_to(c_ref[0, :, ns], (sub, D_STATE)).astype(BF16)
        xd_hi, xd_lo = _split_hi_lo(xd8)
        b_hi, b_lo = _split_hi_lo(b8)
        outer = lambda a, b: lax.dot_general(a, b, contract_first, preferred_element_type=F32)
        upd = outer(xd_hi, b_hi) + outer(xd_lo, b_hi) + outer(xd_hi, b_lo)
        da_hi = da8.astype(BF16)
        da_r = da8 - da_hi.astype(F32)
        da_mid = da_r.astype(BF16)
        da_lo = (da_r - da_mid.astype(F32)).astype(BF16)
        dec = outer(da_hi, ones0) + outer(da_mid, ones0) + outer(da_lo, ones0)
        st_new = st_ref[0, gs, :] * dec + upd
        sn_ref[0, gs, :] = st_new
        yg = lax.dot_general(c8, st_new.astype(BF16), contract_last, preferred_element_type=F32)[0:1, :]
        yg = yg + dsk_ref[:, gs] * xs_ref[0, :, gs]
        yg = yg * _silu(z_ref[0, :, gs])
        yg = yg * lax.rsqrt(jnp.mean(yg * yg, axis=-1, keepdims=True) + EPS)
        y_ref[0, :, gs] = (yg * nrm_ref[:, gs]).astype(y_ref.dtype)


def _ssd_step(xd, da, xs, z, bv, cv, state, dsk_x, nrm):
    n = xd.shape[0]
    r3 = lambda a: a.reshape(n, 1, a.shape[-1])
    wide = pl.BlockSpec((1, 1, D_INNER), lambda b: (b, 0, 0))
    narrow = pl.BlockSpec((1, 1, SSD_GROUPS * D_STATE), lambda b: (b, 0, 0))
    st_spec = pl.BlockSpec((1, D_INNER, D_STATE), lambda b: (b, 0, 0))
    const = pl.BlockSpec((1, D_INNER), lambda b: (0, 0))
    y, st_new = pl.pallas_call(
        _ssd_step_kernel,
        grid=(n,),
        in_specs=[wide, wide, wide, wide, narrow, narrow, st_spec, const, const],
        out_specs=[wide, st_spec],
        out_shape=[jax.ShapeDtypeStruct((n, 1, D_INNER), BF16), jax.ShapeDtypeStruct((n, D_INNER, D_STATE), F32)],
        compiler_params=_params("parallel"),
        name="ssd_step",
    )(r3(xd), r3(da), r3(xs), r3(z), r3(bv), r3(cv), state, dsk_x, nrm)
    return y.reshape(n, D_INNER), st_new


def _alibi_slopes():
    return jnp.exp2(-8.0 * jnp.arange(1, N_HEADS + 1, dtype=F32) / N_HEADS)


def _prep_weights(norm_mix, w_in, conv_w, conv_b, dt_bias, a_log, d_skip, ssd_norm, w_attn_out, w_ssd_out, w_o,
                  norm_ffn, w_ffn_in, w_ffn_out):
    sizes = (2 * D_MODEL, ATTN_WIDTH, KV_WIDTH, KV_WIDTH, D_INNER, CONV_DIM, SSD_HEADS)
    segs, start = [], 0
    for size in sizes:
        segs.append(w_in[:, start:start + size].astype(BF16))
        start += size
    segs[-1] = jnp.pad(segs[-1], ((0, 0), (0, LANES - SSD_HEADS)))
    pad_heads = lambda a: jnp.pad(a.astype(F32), (0, LANES - SSD_HEADS)).reshape(1, LANES)
    head_of_channel = jnp.arange(D_INNER, dtype=jnp.int32) // SSD_HEAD_DIM
    expand = (jnp.arange(LANES, dtype=jnp.int32)[:, None] == head_of_channel[None, :]).astype(BF16)
    return dict(
        norm_mix=norm_mix, w_segs=segs, conv_w=conv_w, conv_b=conv_b.reshape(1, CONV_DIM),
        dt_bias=pad_heads(dt_bias), a_log=pad_heads(a_log),
        d_skip_x=jnp.repeat(d_skip.astype(F32), SSD_HEAD_DIM).reshape(1, D_INNER),
        ssd_norm=ssd_norm.astype(F32).reshape(1, D_INNER), expand=expand,
        w_attn_out=w_attn_out.astype(BF16), w_ssd_out=w_ssd_out.astype(BF16), w_o=w_o.astype(BF16),
        norm_ffn=norm_ffn, w_ffn_in=w_ffn_in.astype(BF16), w_ffn_out=w_ffn_out.astype(BF16),
    )


def _project(x, w):
    hn = _rmsnorm(x, w["norm_mix"])
    names = ("gates", "q", "k", "v", "z", "xbc", "dt")
    return {name: _matmul(hn, seg, name="proj_" + name) for name, seg in zip(names, w["w_segs"])}


def _finish_layer(x, att, y_ssd, gates, w, norm_final):
    mixed = _mix(att, y_ssd, w["w_attn_out"], w["w_ssd_out"], gates)
    x1, hn = _oproj(mixed, w["w_o"], x, w["norm_ffn"])
    h = _ffn_in(hn, w["w_ffn_in"])
    return _ffn_out(h, w["w_ffn_out"], x1, norm_final)


def _prompt_layer(x, w, norm_final, slopes, batch):
    t = x.shape[0] // batch
    p = _project(x, w)
    kmean = _kmean(p["k"], batch)
    att = _moba_prompt(p["q"], p["k"], p["v"], kmean, slopes, batch)
    y_ssd, state = _ssd_prompt(p["xbc"], p["z"], p["dt"], w["conv_w"], w["conv_b"], w["dt_bias"], w["a_log"],
                               w["d_skip_x"], w["ssd_norm"], w["expand"], batch)
    y = _finish_layer(x, att, y_ssd, p["gates"], w, norm_final)
    conv_new = p["xbc"].reshape(batch, t, CONV_DIM)[:, t - (CONV_WIDTH - 1):, :]
    return y, p["k"], p["v"], conv_new, state


def _decode_layer(x, cache_k, cache_v, page_table, conv_state, ssm_state, w, norm_final, slopes):
    n = x.shape[0]
    p = _project(x, w)
    o_part, m_part, l_part, g_part = _dec_attn_partials(p["q"], cache_k, cache_v, page_table, slopes)
    att = _dec_combine(p["q"], p["k"], p["v"], o_part, m_part, l_part, g_part)
    xs, bv, cv, xd, da, conv_new_t = _ssd_step_prep(
        p["xbc"], jnp.transpose(conv_state, (1, 0, 2)), p["dt"], w["conv_w"], w["conv_b"], w["dt_bias"],
        w["a_log"], w["expand"])
    y_ssd, state = _ssd_step(xd, da, xs, p["z"], bv, cv, ssm_state.reshape(n, D_INNER, D_STATE),
                             w["d_skip_x"], w["ssd_norm"])
    y = _finish_layer(x, att, y_ssd, p["gates"], w, norm_final)
    return y, p["k"], p["v"], jnp.transpose(conv_new_t, (1, 0, 2)), state


def kernel(x_prompt, x_sample, cache_k, cache_v, state_conv, state_ssm, page_table, norm_mix, w_in, conv_w, conv_b,
           dt_bias, a_log, d_skip, ssd_norm, w_attn_out, w_ssd_out, w_o, norm_ffn, w_ffn_in, w_ffn_out, norm_final):
    assert w_in.shape[0] == 1, "single-layer kernel"
    bp, t, _ = x_prompt.shape
    nd = x_sample.shape[0]
    assert x_sample.shape[1] == 1
    slopes = _alibi_slopes()
    w = _prep_weights(norm_mix[0], w_in[0], conv_w[0], conv_b[0], dt_bias[0], a_log[0], d_skip[0], ssd_norm[0],
                      w_attn_out[0], w_ssd_out[0], w_o[0], norm_ffn[0], w_ffn_in[0], w_ffn_out[0])

    yp, kp, vp, cp, hp = _prompt_layer(x_prompt.reshape(bp * t, D_MODEL), w, norm_final, slopes, bp)
    ys, ks, vs, cs, hs = _decode_layer(x_sample.reshape(nd, D_MODEL), cache_k[0], cache_v[0], page_table,
                                       state_conv[0], state_ssm[0], w, norm_final, slopes)
    kv = lambda a, b, s: a.reshape(1, b, s, N_KV_HEADS, HEAD_DIM)
    st = lambda a, b: a.reshape(1, b, SSD_HEADS, SSD_HEAD_DIM, D_STATE)
    return (yp.reshape(bp, t, D_MODEL), ys.reshape(nd, 1, D_MODEL),
            kv(kp, bp, t), kv(vp, bp, t), cp[None], st(hp, bp),
            kv(ks, nd, 1), kv(vs, nd, 1), cs[None], st(hs, nd))
```

```python
import functools

import jax
import jax.numpy as jnp
from jax import lax
from jax.experimental import pallas as pl
from jax.experimental.pallas import tpu as pltpu

F32 = jnp.float32
BF16 = jnp.bfloat16

D_MODEL = 2048
N_HEADS = 16
HEAD_DIM = 128
N_KV_HEADS = 4
KV_GROUP = N_HEADS // N_KV_HEADS
ATTN_WIDTH = N_HEADS * HEAD_DIM
KV_WIDTH = N_KV_HEADS * HEAD_DIM
MOBA_BLOCK = 256
MOBA_TOPK = 3
D_INNER = 4096
SSD_HEAD_DIM = 64
SSD_HEADS = D_INNER // SSD_HEAD_DIM
SSD_GROUPS = 8
GROUP_WIDTH = D_INNER // SSD_GROUPS
D_STATE = 128
CONV_WIDTH = 4
CONV_DIM = D_INNER + 2 * SSD_GROUPS * D_STATE
SSD_CHUNK = 128
D_FF = 5632
EPS = 1e-6
LANES = 128
SUBLANES = 8
NEG_BIG = -1e30

VMEM_LIMIT = 48 * 1024 * 1024


def _params(*sem):
    return pltpu.CompilerParams(dimension_semantics=sem, vmem_limit_bytes=VMEM_LIMIT)


def _split_hi_lo(x):
    hi = x.astype(BF16)
    lo = (x - hi.astype(F32)).astype(BF16)
    return hi, lo


def _silu(x):
    return x / (1.0 + jnp.exp(-x))


def _softplus(x):
    return jnp.maximum(x, 0.0) + jnp.log1p(jnp.exp(-jnp.abs(x)))


def _rmsnorm_kernel(x_ref, g_ref, o_ref):
    x = x_ref[...]
    y = x * lax.rsqrt(jnp.mean(x * x, axis=-1, keepdims=True) + EPS)
    o_ref[...] = (y * g_ref[...]).astype(o_ref.dtype)


def _rmsnorm(x, g, out_dtype=BF16, tm=512):
    m, d = x.shape
    tm = min(tm, m)
    return pl.pallas_call(
        _rmsnorm_kernel,
        grid=(pl.cdiv(m, tm),),
        in_specs=[pl.BlockSpec((tm, d), lambda i: (i, 0)), pl.BlockSpec((1, d), lambda i: (0, 0))],
        out_specs=pl.BlockSpec((tm, d), lambda i: (i, 0)),
        out_shape=jax.ShapeDtypeStruct((m, d), out_dtype),
        compiler_params=_params("parallel"),
        name="rmsnorm",
    )(x, g.reshape(1, d))


def _matmul_kernel(a_ref, w_ref, o_ref):
    o_ref[...] = jnp.dot(a_ref[...], w_ref[...], preferred_element_type=F32).astype(o_ref.dtype)


def _matmul(a, w, out_dtype=F32, tm=1024, tn=512, name="matmul"):
    m, k = a.shape
    n = w.shape[1]
    tm, tn = min(tm, m), min(tn, n)
    return pl.pallas_call(
        _matmul_kernel,
        grid=(pl.cdiv(m, tm), n // tn),
        in_specs=[pl.BlockSpec((tm, k), lambda i, j: (i, 0)), pl.BlockSpec((k, tn), lambda i, j: (0, j))],
        out_specs=pl.BlockSpec((tm, tn), lambda i, j: (i, j)),
        out_shape=jax.ShapeDtypeStruct((m, n), out_dtype),
        compiler_params=_params("parallel", "arbitrary"),
        name=name,
    )(a, w)


def _mix_kernel(att_ref, ssd_ref, wa_ref, ws_ref, ga_ref, gs_ref, o_ref):
    ya = jnp.dot(att_ref[...], wa_ref[...], preferred_element_type=F32)
    ys = jnp.dot(ssd_ref[...], ws_ref[...], preferred_element_type=F32)
    o_ref[...] = (jax.nn.sigmoid(ga_ref[...]) * ya + jax.nn.sigmoid(gs_ref[...]) * ys).astype(o_ref.dtype)


def _mix(att, ssd, wa, ws, gates, tm=1024, tn=256):
    m = att.shape[0]
    tm = min(tm, m)
    nj = D_MODEL // tn
    return pl.pallas_call(
        _mix_kernel,
        grid=(pl.cdiv(m, tm), nj),
        in_specs=[
            pl.BlockSpec((tm, ATTN_WIDTH), lambda i, j: (i, 0)),
            pl.BlockSpec((tm, D_INNER), lambda i, j: (i, 0)),
            pl.BlockSpec((ATTN_WIDTH, tn), lambda i, j: (0, j)),
            pl.BlockSpec((D_INNER, tn), lambda i, j: (0, j)),
            pl.BlockSpec((tm, tn), lambda i, j: (i, j)),
            pl.BlockSpec((tm, tn), lambda i, j: (i, j + nj)),
        ],
        out_specs=pl.BlockSpec((tm, tn), lambda i, j: (i, j)),
        out_shape=jax.ShapeDtypeStruct((m, D_MODEL), BF16),
        compiler_params=_params("parallel", "arbitrary"),
        name="mix",
    )(att, ssd, wa, ws, gates, gates)


def _oproj_kernel(mix_ref, wo_ref, x_ref, g_ref, x1_ref, hn_ref):
    x1 = x_ref[...] + jnp.dot(mix_ref[...], wo_ref[...], preferred_element_type=F32)
    x1_ref[...] = x1
    y = x1 * lax.rsqrt(jnp.mean(x1 * x1, axis=-1, keepdims=True) + EPS)
    hn_ref[...] = (y * g_ref[...]).astype(hn_ref.dtype)


def _oproj(mixed, wo, x, g, tm=512):
    m = x.shape[0]
    tm = min(tm, m)
    return pl.pallas_call(
        _oproj_kernel,
        grid=(pl.cdiv(m, tm),),
        in_specs=[
            pl.BlockSpec((tm, D_MODEL), lambda i: (i, 0)),
            pl.BlockSpec((D_MODEL, D_MODEL), lambda i: (0, 0)),
            pl.BlockSpec((tm, D_MODEL), lambda i: (i, 0)),
            pl.BlockSpec((1, D_MODEL), lambda i: (0, 0)),
        ],
        out_specs=[pl.BlockSpec((tm, D_MODEL), lambda i: (i, 0)), pl.BlockSpec((tm, D_MODEL), lambda i: (i, 0))],
        out_shape=[jax.ShapeDtypeStruct((m, D_MODEL), F32), jax.ShapeDtypeStruct((m, D_MODEL), BF16)],
        compiler_params=_params("parallel"),
        name="oproj",
    )(mixed, wo, x, g.reshape(1, D_MODEL))


def _ffn_in_kernel(h_ref, wg_ref, wu_ref, o_ref):
    h = h_ref[...]
    gate = jnp.dot(h, wg_ref[...], preferred_element_type=F32)
    up = jnp.dot(h, wu_ref[...], preferred_element_type=F32)
    o_ref[...] = (_silu(gate) * up).astype(o_ref.dtype)


def _ffn_in(hn, w, tm=1024, tn=512):
    m = hn.shape[0]
    tm = min(tm, m)
    nj = D_FF // tn
    return pl.pallas_call(
        _ffn_in_kernel,
        grid=(pl.cdiv(m, tm), nj),
        in_specs=[
            pl.BlockSpec((tm, D_MODEL), lambda i, j: (i, 0)),
            pl.BlockSpec((D_MODEL, tn), lambda i, j: (0, j)),
            pl.BlockSpec((D_MODEL, tn), lambda i, j: (0, j + nj)),
        ],
        out_specs=pl.BlockSpec((tm, tn), lambda i, j: (i, j)),
        out_shape=jax.ShapeDtypeStruct((m, D_FF), BF16),
        compiler_params=_params("parallel", "arbitrary"),
        name="ffn_in",
    )(hn, w, w)


def _ffn_out_kernel(h_ref, w_ref, x_ref, g_ref, o_ref):
    k = pl.program_id(1)

    @pl.when(k == 0)
    def _():
        o_ref[...] = x_ref[...]

    o_ref[...] += jnp.dot(h_ref[...], w_ref[...], preferred_element_type=F32)

    @pl.when(k == pl.num_programs(1) - 1)
    def _():
        x2 = o_ref[...]
        y = x2 * lax.rsqrt(jnp.mean(x2 * x2, axis=-1, keepdims=True) + EPS)
        o_ref[...] = y * g_ref[...]


def _ffn_out(h, w, x1, g, tm=512, tk=512):
    m = h.shape[0]
    tm = min(tm, m)
    return pl.pallas_call(
        _ffn_out_kernel,
        grid=(pl.cdiv(m, tm), D_FF // tk),
        in_specs=[
            pl.BlockSpec((tm, tk), lambda i, k: (i, k)),
            pl.BlockSpec((tk, D_MODEL), lambda i, k: (k, 0)),
            pl.BlockSpec((tm, D_MODEL), lambda i, k: (i, 0)),
            pl.BlockSpec((1, D_MODEL), lambda i, k: (0, 0)),
        ],
        out_specs=pl.BlockSpec((tm, D_MODEL), lambda i, k: (i, 0)),
        out_shape=jax.ShapeDtypeStruct((m, D_MODEL), F32),
        compiler_params=_params("parallel", "arbitrary"),
        name="ffn_out",
    )(h, w, x1, g.reshape(1, D_MODEL))


def _kmean_kernel(k_ref, o_ref):
    t = k_ref.shape[0]
    nb = t // MOBA_BLOCK
    km = jnp.mean(k_ref[...].reshape(nb, MOBA_BLOCK, KV_WIDTH), axis=1)
    o_ref[0] = jnp.concatenate([km, jnp.zeros((LANES - nb, KV_WIDTH), F32)], axis=0)


def _kmean(k, batch):
    t = k.shape[0] // batch
    return pl.pallas_call(
        _kmean_kernel,
        grid=(batch,),
        in_specs=[pl.BlockSpec((t, KV_WIDTH), lambda b: (b, 0))],
        out_specs=pl.BlockSpec((1, LANES, KV_WIDTH), lambda b: (b, 0, 0)),
        out_shape=jax.ShapeDtypeStruct((batch, LANES, KV_WIDTH), F32),
        compiler_params=_params("parallel"),
        name="kmean",
    )(k)


def _top_rows(gate, row_id, n_valid, n_pick):
    picks = []
    for t in range(n_pick):
        mx = jnp.max(gate, axis=0, keepdims=True)
        idx = jnp.min(jnp.where(gate == mx, row_id, jnp.iinfo(jnp.int32).max), axis=0, keepdims=True)
        picks.append(jnp.where(t < n_valid, idx, -1))
        gate = jnp.where(row_id == idx, -jnp.inf, gate)
    return picks


def _moba_kernel(slopes_ref, q_ref, k_ref, v_ref, km_ref, o_ref, kb_ref, vt_ref, bias_ref, m_ref, l_ref, acc_ref):
    kvh = pl.program_id(1)
    i = pl.program_id(2)
    nq = KV_GROUP * MOBA_BLOCK
    nb = vt_ref.shape[0]
    nb_rows = -(-nb // 8) * 8

    @pl.when(i == 0)
    def _():
        kb_ref[...] = k_ref[...].astype(BF16)
        for n in range(nb):
            vt_ref[n] = v_ref[n * MOBA_BLOCK:(n + 1) * MOBA_BLOCK, :].T.astype(BF16)

    q = q_ref[...]
    qs = jnp.concatenate([q[:, g * HEAD_DIM:(g + 1) * HEAD_DIM] for g in range(KV_GROUP)], axis=0)
    qsb = (qs * (HEAD_DIM ** -0.5)).astype(BF16)

    contract_last = (((1,), (1,)), ((), ()))
    q_hi, q_lo = _split_hi_lo(qs)
    km_hi, km_lo = _split_hi_lo(km_ref[0, 0:nb_rows, :])
    gate = (lax.dot_general(km_hi, q_hi, contract_last, preferred_element_type=F32)
            + lax.dot_general(km_hi, q_lo, contract_last, preferred_element_type=F32)
            + lax.dot_general(km_lo, q_hi, contract_last, preferred_element_type=F32))
    blk = lax.broadcasted_iota(jnp.int32, (nb_rows, nq), 0)
    picks = _top_rows(jnp.where(blk < i, gate, -jnp.inf), blk, i, MOBA_TOPK)

    qlane = lax.broadcasted_iota(jnp.int32, (1, nq), 1)
    slope = jnp.zeros((1, nq), F32)
    for g in range(KV_GROUP):
        slope = jnp.where(qlane // MOBA_BLOCK == g, slopes_ref[kvh * KV_GROUP + g], slope)
    kidx = lax.broadcasted_iota(jnp.int32, (MOBA_BLOCK, nq), 0)
    bias_ref[...] = kidx.astype(F32) * slope

    m_ref[...] = jnp.full((1, nq), NEG_BIG, F32)
    l_ref[...] = jnp.zeros((1, nq), F32)
    acc_ref[...] = jnp.zeros((HEAD_DIM, nq), F32)

    def past_block(n, carry):
        start = pl.multiple_of(n * MOBA_BLOCK, MOBA_BLOCK)
        kb = kb_ref[pl.ds(start, MOBA_BLOCK), :]
        u = lax.dot_general(kb, qsb, contract_last, preferred_element_type=F32) + bias_ref[...]
        c = ((i - n) * MOBA_BLOCK).astype(F32) * slope
        sel = (picks[0] == n) | (picks[1] == n) | (picks[2] == n)
        m_old = m_ref[...]
        m_new = jnp.where(sel, jnp.maximum(m_old, jnp.max(u, axis=0, keepdims=True) - c), m_old)
        p = jnp.exp(u - (jnp.where(sel, m_new, -NEG_BIG) + c))
        alpha = jnp.exp(m_old - m_new)
        l_ref[...] = alpha * l_ref[...] + jnp.sum(p, axis=0, keepdims=True)
        acc_ref[...] = alpha * acc_ref[...] + jnp.dot(vt_ref[n], p.astype(BF16), preferred_element_type=F32)
        m_ref[...] = m_new
        return carry

    lax.fori_loop(0, i, past_block, 0)

    start = pl.multiple_of(i * MOBA_BLOCK, MOBA_BLOCK)
    kb = kb_ref[pl.ds(start, MOBA_BLOCK), :]
    u = lax.dot_general(kb, qsb, contract_last, preferred_element_type=F32) + bias_ref[...]
    u = jnp.where(kidx <= qlane % MOBA_BLOCK, u, NEG_BIG)
    m_old = m_ref[...]
    m_new = jnp.maximum(m_old, jnp.max(u, axis=0, keepdims=True))
    p = jnp.exp(u - m_new)
    alpha = jnp.exp(m_old - m_new)
    l = alpha * l_ref[...] + jnp.sum(p, axis=0, keepdims=True)
    acc = alpha * acc_ref[...] + jnp.dot(vt_ref[i], p.astype(BF16), preferred_element_type=F32)
    out = acc / l
    for g in range(KV_GROUP):
        o_ref[:, g * HEAD_DIM:(g + 1) * HEAD_DIM] = out[:, g * MOBA_BLOCK:(g + 1) * MOBA_BLOCK].T.astype(o_ref.dtype)


def _moba_prompt(q, k, v, kmean, slopes, batch):
    t = q.shape[0] // batch
    nb = t // MOBA_BLOCK
    assert nb <= LANES
    nq = KV_GROUP * MOBA_BLOCK
    grid_spec = pltpu.PrefetchScalarGridSpec(
        num_scalar_prefetch=1,
        grid=(batch, N_KV_HEADS, nb),
        in_specs=[
            pl.BlockSpec((MOBA_BLOCK, KV_GROUP * HEAD_DIM), lambda b, h, i, s: (b * nb + i, h)),
            pl.BlockSpec((t, HEAD_DIM), lambda b, h, i, s: (b, h)),
            pl.BlockSpec((t, HEAD_DIM), lambda b, h, i, s: (b, h)),
            pl.BlockSpec((1, LANES, HEAD_DIM), lambda b, h, i, s: (b, 0, h)),
        ],
        out_specs=pl.BlockSpec((MOBA_BLOCK, KV_GROUP * HEAD_DIM), lambda b, h, i, s: (b * nb + i, h)),
        scratch_shapes=[
            pltpu.VMEM((t, HEAD_DIM), BF16),
            pltpu.VMEM((nb, HEAD_DIM, MOBA_BLOCK), BF16),
            pltpu.VMEM((MOBA_BLOCK, nq), F32),
            pltpu.VMEM((1, nq), F32),
            pltpu.VMEM((1, nq), F32),
            pltpu.VMEM((HEAD_DIM, nq), F32),
        ],
    )
    return pl.pallas_call(
        _moba_kernel,
        grid_spec=grid_spec,
        out_shape=jax.ShapeDtypeStruct((batch * t, ATTN_WIDTH), BF16),
        compiler_params=_params("arbitrary", "arbitrary", "arbitrary"),
        name="moba_prompt",
    )(slopes, q, k, v, kmean)


def _ssd_prompt_kernel(xbc_ref, z_ref, dt_ref, cw_ref, cb_ref, dtb_ref, alog_ref, dsk_ref, nrm_ref, e_ref,
                       y_ref, st_ref, xpad_ref):
    c = pl.program_id(1)
    L = SSD_CHUNK
    head = 8

    @pl.when(c == 0)
    def _():
        st_ref[...] = jnp.zeros_like(st_ref)
        xpad_ref[0:head, :] = jnp.zeros((head, CONV_DIM), F32)

    xpad_ref[head:head + L, :] = xbc_ref[...]
    off = head - (CONV_WIDTH - 1)
    conv = cb_ref[...] + xpad_ref[off:off + L, :] * cw_ref[0:1, :]
    for i in range(1, CONV_WIDTH):
        conv = conv + xpad_ref[off + i:off + i + L, :] * cw_ref[i:i + 1, :]
    xpad_ref[0:head, :] = xpad_ref[L:L + head, :]
    act = _silu(conv)
    xs = act[:, :D_INNER]
    bm = act[:, D_INNER:D_INNER + SSD_GROUPS * D_STATE]
    cm = act[:, D_INNER + SSD_GROUPS * D_STATE:]

    dt = _softplus(dt_ref[...] + dtb_ref[...])
    la = dt * (-jnp.exp(alog_ref[...]))
    r_i = lax.broadcasted_iota(jnp.int32, (L, L), 0)
    c_i = lax.broadcasted_iota(jnp.int32, (L, L), 1)
    causal = r_i >= c_i
    tri = causal.astype(BF16)
    la_hi = la.astype(BF16)
    la_r = la - la_hi.astype(F32)
    la_mid = la_r.astype(BF16)
    la_lo = (la_r - la_mid.astype(F32)).astype(BF16)
    acs = (jnp.dot(tri, la_hi, preferred_element_type=F32) + jnp.dot(tri, la_mid, preferred_element_type=F32)
           + jnp.dot(tri, la_lo, preferred_element_type=F32))
    acs_t = acs.T
    dt_t = dt.T
    acs_last = acs[L - 1:L, :]
    e = e_ref[...]
    exp_acs_x = jnp.dot(jnp.exp(acs).astype(BF16), e, preferred_element_type=F32)
    wend_x = jnp.dot((dt * jnp.exp(acs_last - acs)).astype(BF16), e, preferred_element_type=F32)
    dec_last = jnp.exp(acs_t[:, L - 1:L])
    lane = lax.broadcasted_iota(jnp.int32, (L, LANES), 1)
    contract_last = (((1,), (1,)), ((), ()))
    contract_first = (((0,), (0,)), ((), ()))
    z = z_ref[...]

    for g in range(SSD_GROUPS):
        gs = slice(g * GROUP_WIDTH, (g + 1) * GROUP_WIDTH)
        bg = bm[:, g * D_STATE:(g + 1) * D_STATE].astype(BF16)
        cg = cm[:, g * D_STATE:(g + 1) * D_STATE].astype(BF16)
        cb = lax.dot_general(cg, bg, contract_last, preferred_element_type=F32)
        st = st_ref[0, gs, :]
        y_state = lax.dot_general(cg, st.astype(BF16), contract_last, preferred_element_type=F32)
        pairs = []
        for j in range(GROUP_WIDTH // LANES):
            col = g * (GROUP_WIDTH // LANES) + j
            ms = []
            for h in (2 * col, 2 * col + 1):
                seg = acs[:, h:h + 1] - acs_t[h:h + 1, :]
                ms.append(cb * jnp.exp(jnp.where(causal, seg, -jnp.inf)) * dt_t[h:h + 1, :])
            xcol = xs[:, col * LANES:(col + 1) * LANES]
            rhs = jnp.concatenate([jnp.where(lane < SSD_HEAD_DIM, xcol, 0.0),
                                   jnp.where(lane >= SSD_HEAD_DIM, xcol, 0.0)], axis=0).astype(BF16)
            lhs = jnp.concatenate(ms, axis=1).astype(BF16)
            pairs.append(jnp.dot(lhs, rhs, preferred_element_type=F32))
        xg = xs[:, gs]
        yg = jnp.concatenate(pairs, axis=1) + y_state * exp_acs_x[:, gs] + dsk_ref[:, gs] * xg
        yg = yg * _silu(z[:, gs])
        yg = yg * lax.rsqrt(jnp.mean(yg * yg, axis=-1, keepdims=True) + EPS)
        y_ref[:, gs] = (yg * nrm_ref[:, gs]).astype(y_ref.dtype)

        upd = lax.dot_general((xg * wend_x[:, gs]).astype(BF16), bg, contract_first, preferred_element_type=F32)
        drows = jnp.concatenate(
            [jnp.broadcast_to(dec_last[h:h + 1, :], (SSD_HEAD_DIM, D_STATE))
             for h in range(g * 8, (g + 1) * 8)], axis=0)
        st_ref[0, gs, :] = st * drows + upd


def _ssd_prompt(xbc, z, dt_raw, cw, cb, dtb, alog, dsk_x, nrm, e, batch):
    t = xbc.shape[0] // batch
    nc = t // SSD_CHUNK
    row = lambda b, c: (b * nc + c, 0)
    const = lambda b, c: (0, 0)
    return pl.pallas_call(
        _ssd_prompt_kernel,
        grid=(batch, nc),
        in_specs=[
            pl.BlockSpec((SSD_CHUNK, CONV_DIM), row),
            pl.BlockSpec((SSD_CHUNK, D_INNER), row),
            pl.BlockSpec((SSD_CHUNK, LANES), row),
            pl.BlockSpec((CONV_WIDTH, CONV_DIM), const),
            pl.BlockSpec((1, CONV_DIM), const),
            pl.BlockSpec((1, LANES), const),
            pl.BlockSpec((1, LANES), const),
            pl.BlockSpec((1, D_INNER), const),
            pl.BlockSpec((1, D_INNER), const),
            pl.BlockSpec((LANES, D_INNER), const),
        ],
        out_specs=[
            pl.BlockSpec((SSD_CHUNK, D_INNER), row),
            pl.BlockSpec((1, D_INNER, D_STATE), lambda b, c: (b, 0, 0)),
        ],
        out_shape=[
            jax.ShapeDtypeStruct((batch * t, D_INNER), BF16),
            jax.ShapeDtypeStruct((batch, D_INNER, D_STATE), F32),
        ],
        scratch_shapes=[pltpu.VMEM((SSD_CHUNK + 8, CONV_DIM), F32)],
        compiler_params=_params("arbitrary", "arbitrary"),
        name="ssd_prompt",
    )(xbc, z, dt_raw, cw, cb, dtb, alog, dsk_x, nrm, e)


KMEAN_PAGES_PER_STEP = 16


def _dec_kmean_kernel(pt_ref, *refs):
    k_refs, o_ref = refs[:-1], refs[-1]
    rows = k_refs[0].shape[1]
    pages_per_block = MOBA_BLOCK * N_KV_HEADS // rows
    for j in range(len(k_refs) // pages_per_block):
        acc = jnp.zeros((SUBLANES, HEAD_DIM), F32)
        for page in k_refs[j * pages_per_block:(j + 1) * pages_per_block]:
            acc = acc + jnp.sum(page[0].reshape(rows // SUBLANES, SUBLANES, HEAD_DIM), axis=0)
        km = (acc[0:N_KV_HEADS] + acc[N_KV_HEADS:SUBLANES]) * (1.0 / MOBA_BLOCK)
        for c in range(N_KV_HEADS):
            o_ref[0, c, j:j + 1, :] = km[c:c + 1, :]


def _dec_kmean(kc, page_table, nb):
    nseq, n_pages = page_table.shape
    pps = min(KMEAN_PAGES_PER_STEP, n_pages)
    rows = kc.shape[1]
    bps = pps * rows // (MOBA_BLOCK * N_KV_HEADS)
    page_spec = lambda j: pl.BlockSpec((1, rows, HEAD_DIM), lambda b, s, pt: (pt[b, s * pps + j], 0, 0))
    grid_spec = pltpu.PrefetchScalarGridSpec(
        num_scalar_prefetch=1,
        grid=(nseq, n_pages // pps),
        in_specs=[page_spec(j) for j in range(pps)],
        out_specs=pl.BlockSpec((1, N_KV_HEADS, bps, HEAD_DIM), lambda b, s, pt: (b, 0, s, 0)),
    )
    return pl.pallas_call(
        _dec_kmean_kernel,
        grid_spec=grid_spec,
        out_shape=jax.ShapeDtypeStruct((nseq, N_KV_HEADS, nb, HEAD_DIM), F32),
        compiler_params=_params("arbitrary", "arbitrary"),
        name="dec_kmean",
    )(page_table, *([kc] * pps))


def _dec_select_kernel(q_ref, km_ref, sel_ref):
    nb = km_ref.shape[2]
    hrow = lax.broadcasted_iota(jnp.int32, (N_HEADS, 1), 0)
    contract_last = (((1,), (1,)), ((), ()))
    q_hi, q_lo = _split_hi_lo(q_ref[0])
    gate = jnp.zeros((N_HEADS, nb), F32)
    for c in range(N_KV_HEADS):
        km_hi, km_lo = _split_hi_lo(km_ref[0, c])
        g = (lax.dot_general(q_hi, km_hi, contract_last, preferred_element_type=F32)
             + lax.dot_general(q_lo, km_hi, contract_last, preferred_element_type=F32)
             + lax.dot_general(q_hi, km_lo, contract_last, preferred_element_type=F32))
        gate = jnp.where(hrow // KV_GROUP == c, g, gate)
    blk = lax.broadcasted_iota(jnp.int32, (N_HEADS, nb), 1)
    lane = lax.broadcasted_iota(jnp.int32, (N_HEADS, LANES), 1)
    sel = jnp.zeros((N_HEADS, LANES), jnp.int32)
    for t in range(MOBA_TOPK):
        mx = jnp.max(gate, axis=-1, keepdims=True)
        idx = jnp.min(jnp.where(gate == mx, blk, nb), axis=-1, keepdims=True)
        sel = jnp.where(lane == t, idx, sel)
        gate = jnp.where(blk == idx, -jnp.inf, gate)
    sel_ref[0] = sel


def _dec_select(q, kmean):
    nseq, _, nb, _ = kmean.shape
    assert nb >= MOBA_TOPK
    return pl.pallas_call(
        _dec_select_kernel,
        grid=(nseq,),
        in_specs=[pl.BlockSpec((1, N_HEADS, HEAD_DIM), lambda b: (b, 0, 0)),
                  pl.BlockSpec((1, N_KV_HEADS, nb, HEAD_DIM), lambda b: (b, 0, 0, 0))],
        out_specs=pl.BlockSpec((1, N_HEADS, LANES), lambda b: (b, 0, 0)),
        out_shape=jax.ShapeDtypeStruct((nseq, N_HEADS, LANES), jnp.int32),
        compiler_params=_params("parallel"),
        name="dec_select",
    )(q.reshape(nseq, N_HEADS, HEAD_DIM), kmean)


def _dec_attn_kernel(pt_ref, sel_ref, slopes_ref, q_ref, kn_ref, vn_ref, *refs, past_len, pages_per_block):
    n_sel = MOBA_TOPK * pages_per_block
    k_refs, v_refs, o_ref = refs[:n_sel], refs[n_sel:2 * n_sel], refs[2 * n_sel]
    b = pl.program_id(0)
    h = pl.program_id(1)
    c = h // KV_GROUP
    rows = k_refs[0].shape[1]
    keys_per_page = rows // N_KV_HEADS
    scale = HEAD_DIM ** -0.5
    qh = q_ref[0, pl.ds(h, 1), :]
    slope = slopes_ref[h]
    r = lax.broadcasted_iota(jnp.int32, (rows, 1), 0)
    mine = r % N_KV_HEADS == c
    scores = []
    for t in range(MOBA_TOPK):
        blk = sel_ref[b, h * MOBA_TOPK + t]
        for j in range(pages_per_block):
            s = jnp.sum(k_refs[t * pages_per_block + j][0] * qh, axis=-1, keepdims=True) * scale
            kpos = blk * MOBA_BLOCK + j * keys_per_page + r // N_KV_HEADS
            s = s - slope * (past_len - kpos).astype(F32)
            scores.append(jnp.where(mine, s, NEG_BIG))
    kn = kn_ref[0, pl.ds(c, 1), :]
    vn = vn_ref[0, pl.ds(c, 1), :]
    s_self = jnp.sum(qh * kn, axis=-1, keepdims=True) * scale
    m = s_self
    for s in scores:
        m = jnp.maximum(m, jnp.max(s, axis=0, keepdims=True))
    w_self = jnp.exp(s_self - m)
    l = w_self
    o = w_self * vn
    for s, v_ref in zip(scores, v_refs):
        p = jnp.exp(s - m)
        l = l + jnp.sum(p, axis=0, keepdims=True)
        o = o + jnp.sum(p * v_ref[0], axis=0, keepdims=True)
    o_ref[0, 0] = o / l


def _dec_attn(q, k_new, v_new, kc, vc, page_table, sel, slopes):
    nseq, n_pages = page_table.shape
    rows = kc.shape[1]
    ppb = MOBA_BLOCK * N_KV_HEADS // rows
    page_spec = lambda t, j: pl.BlockSpec(
        (1, rows, HEAD_DIM), lambda b, h, pt, sel, sl: (pt[b, ppb * sel[b, h * MOBA_TOPK + t] + j], 0, 0))
    page_specs = [page_spec(t, j) for t in range(MOBA_TOPK) for j in range(ppb)]
    kv_spec = pl.BlockSpec((1, N_KV_HEADS, HEAD_DIM), lambda b, h, pt, sel, sl: (b, 0, 0))
    grid_spec = pltpu.PrefetchScalarGridSpec(
        num_scalar_prefetch=3,
        grid=(nseq, N_HEADS),
        in_specs=[pl.BlockSpec((1, N_HEADS, HEAD_DIM), lambda b, h, pt, sel, sl: (b, 0, 0)), kv_spec, kv_spec]
        + page_specs + page_specs,
        out_specs=pl.BlockSpec((1, 1, 1, HEAD_DIM), lambda b, h, pt, sel, sl: (b, h, 0, 0)),
    )
    n_sel = len(page_specs)
    out = pl.pallas_call(
        functools.partial(_dec_attn_kernel, past_len=n_pages * rows // N_KV_HEADS, pages_per_block=ppb),
        grid_spec=grid_spec,
        out_shape=jax.ShapeDtypeStruct((nseq, N_HEADS, 1, HEAD_DIM), F32),
        compiler_params=_params("arbitrary", "arbitrary"),
        name="dec_attn",
    )(page_table, sel, slopes, q.reshape(nseq, N_HEADS, HEAD_DIM), k_new.reshape(nseq, N_KV_HEADS, HEAD_DIM),
      v_new.reshape(nseq, N_KV_HEADS, HEAD_DIM), *([kc] * n_sel), *([vc] * n_sel))
    return out.reshape(nseq, ATTN_WIDTH).astype(BF16)


def _ssd_step_prep_kernel(xbc_ref, cs_ref, dt_ref, cw_ref, cb_ref, dtb_ref, alog_ref, e_ref,
                          xs_ref, b_ref, c_ref, xd_ref, da_ref, cn_ref):
    x = xbc_ref[...]
    conv = cb_ref[...] + cs_ref[0] * cw_ref[0:1, :]
    for i in range(1, CONV_WIDTH - 1):
        conv = conv + cs_ref[i] * cw_ref[i:i + 1, :]
    conv = conv + x * cw_ref[CONV_WIDTH - 1:CONV_WIDTH, :]
    for i in range(CONV_WIDTH - 2):
        cn_ref[i] = cs_ref[i + 1]
    cn_ref[CONV_WIDTH - 2] = x
    act = _silu(conv)
    xs = act[:, :D_INNER]
    xs_ref[...] = xs
    b_ref[...] = act[:, D_INNER:D_INNER + SSD_GROUPS * D_STATE]
    c_ref[...] = act[:, D_INNER + SSD_GROUPS * D_STATE:]
    dt = _softplus(dt_ref[...] + dtb_ref[...])
    da = jnp.exp(dt * (-jnp.exp(alog_ref[...])))
    e = e_ref[...]
    dt_hi, dt_lo = _split_hi_lo(dt)
    da_hi, da_lo = _split_hi_lo(da)
    dt_x = jnp.dot(dt_hi, e, preferred_element_type=F32) + jnp.dot(dt_lo, e, preferred_element_type=F32)
    da_ref[...] = jnp.dot(da_hi, e, preferred_element_type=F32) + jnp.dot(da_lo, e, preferred_element_type=F32)
    xd_ref[...] = xs * dt_x


def _ssd_step_prep(xbc, conv_state_t, dt_raw, cw, cb, dtb, alog, e):
    n = xbc.shape[0]
    sds = lambda *shape: jax.ShapeDtypeStruct(shape, F32)
    return pl.pallas_call(
        _ssd_step_prep_kernel,
        out_shape=[sds(n, D_INNER), sds(n, SSD_GROUPS * D_STATE), sds(n, SSD_GROUPS * D_STATE),
                   sds(n, D_INNER), sds(n, D_INNER), sds(CONV_WIDTH - 1, n, CONV_DIM)],
        compiler_params=pltpu.CompilerParams(vmem_limit_bytes=VMEM_LIMIT),
        name="ssd_step_prep",
    )(xbc, conv_state_t, dt_raw, cw, cb, dtb, alog, e)


def _ssd_step_kernel(xd_ref, da_ref, xs_ref, z_ref, b_ref, c_ref, st_ref, dsk_ref, nrm_ref, y_ref, sn_ref):
    sub = 8
    row0 = lax.broadcasted_iota(jnp.int32, (sub, D_STATE), 0) == 0
    contract_last = (((1,), (1,)), ((), ()))
    contract_first = (((0,), (0,)), ((), ()))
    ones0 = jnp.where(row0, 1.0, 0.0).astype(BF16)
    for g in range(SSD_GROUPS):
        gs = slice(g * GROUP_WIDTH, (g + 1) * GROUP_WIDTH)
        ns = slice(g * D_STATE, (g + 1) * D_STATE)
        xd8 = jnp.broadcast_to(xd_ref[0, :, gs], (sub, GROUP_WIDTH))
        da8 = jnp.broadcast_to(da_ref[0, :, gs], (sub, GROUP_WIDTH))
        b8 = jnp.where(row0, jnp.broadcast_to(b_ref[0, :, ns], (sub, D_STATE)), 0.0)
        c8 = jnp.broadcast_to(c_ref[0, :, ns], (sub, D_STATE)).astype(BF16)
        xd_hi, xd_lo = _split_hi_lo(xd8)
        b_hi, b_lo = _split_hi_lo(b8)
        outer = lambda a, b: lax.dot_general(a, b, contract_first, preferred_element_type=F32)
        upd = outer(xd_hi, b_hi) + outer(xd_lo, b_hi) + outer(xd_hi, b_lo)
        da_hi = da8.astype(BF16)
        da_r = da8 - da_hi.astype(F32)
        da_mid = da_r.astype(BF16)
        da_lo = (da_r - da_mid.astype(F32)).astype(BF16)
        dec = outer(da_hi, ones0) + outer(da_mid, ones0) + outer(da_lo, ones0)
        st_new = st_ref[0, gs, :] * dec + upd
        sn_ref[0, gs, :] = st_new
        yg = lax.dot_general(c8, st_new.astype(BF16), contract_last, preferred_element_type=F32)[0:1, :]
        yg = yg + dsk_ref[:, gs] * xs_ref[0, :, gs]
        yg = yg * _silu(z_ref[0, :, gs])
        yg = yg * lax.rsqrt(jnp.mean(yg * yg, axis=-1, keepdims=True) + EPS)
        y_ref[0, :, gs] = (yg * nrm_ref[:, gs]).astype(y_ref.dtype)


def _ssd_step(xd, da, xs, z, bv, cv, state, dsk_x, nrm):
    n = xd.shape[0]
    r3 = lambda a: a.reshape(n, 1, a.shape[-1])
    wide = pl.BlockSpec((1, 1, D_INNER), lambda b: (b, 0, 0))
    narrow = pl.BlockSpec((1, 1, SSD_GROUPS * D_STATE), lambda b: (b, 0, 0))
    st_spec = pl.BlockSpec((1, D_INNER, D_STATE), lambda b: (b, 0, 0))
    const = pl.BlockSpec((1, D_INNER), lambda b: (0, 0))
    y, st_new = pl.pallas_call(
        _ssd_step_kernel,
        grid=(n,),
        in_specs=[wide, wide, wide, wide, narrow, narrow, st_spec, const, const],
        out_specs=[wide, st_spec],
        out_shape=[jax.ShapeDtypeStruct((n, 1, D_INNER), BF16), jax.ShapeDtypeStruct((n, D_INNER, D_STATE), F32)],
        compiler_params=_params("parallel"),
        name="ssd_step",
    )(r3(xd), r3(da), r3(xs), r3(z), r3(bv), r3(cv), state, dsk_x, nrm)
    return y.reshape(n, D_INNER), st_new


def _alibi_slopes():
    return jnp.exp2(-8.0 * jnp.arange(1, N_HEADS + 1, dtype=F32) / N_HEADS)


def _prep_weights(norm_mix, w_in, conv_w, conv_b, dt_bias, a_log, d_skip, ssd_norm, w_attn_out, w_ssd_out, w_o,
                  norm_ffn, w_ffn_in, w_ffn_out):
    sizes = (2 * D_MODEL, ATTN_WIDTH, KV_WIDTH, KV_WIDTH, D_INNER, CONV_DIM, SSD_HEADS)
    segs, start = [], 0
    for size in sizes:
        segs.append(w_in[:, start:start + size].astype(BF16))
        start += size
    segs[-1] = jnp.pad(segs[-1], ((0, 0), (0, LANES - SSD_HEADS)))
    pad_heads = lambda a: jnp.pad(a.astype(F32), (0, LANES - SSD_HEADS)).reshape(1, LANES)
    head_of_channel = jnp.arange(D_INNER, dtype=jnp.int32) // SSD_HEAD_DIM
    expand = (jnp.arange(LANES, dtype=jnp.int32)[:, None] == head_of_channel[None, :]).astype(BF16)
    return dict(
        norm_mix=norm_mix, w_segs=segs, conv_w=conv_w, conv_b=conv_b.reshape(1, CONV_DIM),
        dt_bias=pad_heads(dt_bias), a_log=pad_heads(a_log),
        d_skip_x=jnp.repeat(d_skip.astype(F32), SSD_HEAD_DIM).reshape(1, D_INNER),
        ssd_norm=ssd_norm.astype(F32).reshape(1, D_INNER), expand=expand,
        w_attn_out=w_attn_out.astype(BF16), w_ssd_out=w_ssd_out.astype(BF16), w_o=w_o.astype(BF16),
        norm_ffn=norm_ffn, w_ffn_in=w_ffn_in.astype(BF16), w_ffn_out=w_ffn_out.astype(BF16),
    )


def _project(x, w):
    hn = _rmsnorm(x, w["norm_mix"])
    names = ("gates", "q", "k", "v", "z", "xbc", "dt")
    return {name: _matmul(hn, seg, name="proj_" + name) for name, seg in zip(names, w["w_segs"])}


def _finish_layer(x, att, y_ssd, gates, w, norm_final):
    mixed = _mix(att, y_ssd, w["w_attn_out"], w["w_ssd_out"], gates)
    x1, hn = _oproj(mixed, w["w_o"], x, w["norm_ffn"])
    h = _ffn_in(hn, w["w_ffn_in"])
    return _ffn_out(h, w["w_ffn_out"], x1, norm_final)


def _prompt_layer(x, w, norm_final, slopes, batch):
    t = x.shape[0] // batch
    p = _project(x, w)
    kmean = _kmean(p["k"], batch)
    att = _moba_prompt(p["q"], p["k"], p["v"], kmean, slopes, batch)
    y_ssd, state = _ssd_prompt(p["xbc"], p["z"], p["dt"], w["conv_w"], w["conv_b"], w["dt_bias"], w["a_log"],
                               w["d_skip_x"], w["ssd_norm"], w["expand"], batch)
    y = _finish_layer(x, att, y_ssd, p["gates"], w, norm_final)
    conv_new = p["xbc"].reshape(batch, t, CONV_DIM)[:, t - (CONV_WIDTH - 1):, :]
    return y, p["k"], p["v"], conv_new, state


def _decode_layer(x, cache_k, cache_v, page_table, conv_state, ssm_state, w, norm_final, slopes):
    n = x.shape[0]
    p = _project(x, w)
    n_phys, page = cache_k.shape[0], cache_k.shape[1]
    kc = cache_k.reshape(n_phys, page * N_KV_HEADS, HEAD_DIM)
    vc = cache_v.reshape(n_phys, page * N_KV_HEADS, HEAD_DIM)
    nb = page_table.shape[1] * page // MOBA_BLOCK
    sel = _dec_select(p["q"], _dec_kmean(kc, page_table, nb))
    att = _dec_attn(p["q"], p["k"], p["v"], kc, vc, page_table,
                    sel[:, :, :MOBA_TOPK].reshape(n, N_HEADS * MOBA_TOPK), slopes)
    xs, bv, cv, xd, da, conv_new_t = _ssd_step_prep(
        p["xbc"], jnp.transpose(conv_state, (1, 0, 2)), p["dt"], w["conv_w"], w["conv_b"], w["dt_bias"],
        w["a_log"], w["expand"])
    y_ssd, state = _ssd_step(xd, da, xs, p["z"], bv, cv, ssm_state.reshape(n, D_INNER, D_STATE),
                             w["d_skip_x"], w["ssd_norm"])
    y = _finish_layer(x, att, y_ssd, p["gates"], w, norm_final)
    return y, p["k"], p["v"], jnp.transpose(conv_new_t, (1, 0, 2)), state


def kernel(x_prompt, x_sample, cache_k, cache_v, state_conv, state_ssm, page_table, norm_mix, w_in, conv_w, conv_b,
           dt_bias, a_log, d_skip, ssd_norm, w_attn_out, w_ssd_out, w_o, norm_ffn, w_ffn_in, w_ffn_out, norm_final):
    assert w_in.shape[0] == 1, "single-layer kernel"
    bp, t, _ = x_prompt.shape
    nd = x_sample.shape[0]
    assert x_sample.shape[1] == 1
    slopes = _alibi_slopes()
    w = _prep_weights(norm_mix[0], w_in[0], conv_w[0], conv_b[0], dt_bias[0], a_log[0], d_skip[0], ssd_norm[0],
                      w_attn_out[0], w_ssd_out[0], w_o[0], norm_ffn[0], w_ffn_in[0], w_ffn_out[0])

    yp, kp, vp, cp, hp = _prompt_layer(x_prompt.reshape(bp * t, D_MODEL), w, norm_final, slopes, bp)
    ys, ks, vs, cs, hs = _decode_layer(x_sample.reshape(nd, D_MODEL), cache_k[0], cache_v[0], page_table,
                                       state_conv[0], state_ssm[0], w, norm_final, slopes)
    kv = lambda a, b, s: a.reshape(1, b, s, N_KV_HEADS, HEAD_DIM)
    st = lambda a, b: a.reshape(1, b, SSD_HEADS, SSD_HEAD_DIM, D_STATE)
    return (yp.reshape(bp, t, D_MODEL), ys.reshape(nd, 1, D_MODEL),
            kv(kp, bp, t), kv(vp, bp, t), cp[None], st(hp, bp),
            kv(ks, nd, 1), kv(vs, nd, 1), cs[None], st(hs, nd))
```

```python
import functools

import jax
import jax.numpy as jnp
from jax import lax
from jax.experimental import pallas as pl
from jax.experimental.pallas import tpu as pltpu

F32 = jnp.float32
BF16 = jnp.bfloat16

D_MODEL = 2048
N_HEADS = 16
HEAD_DIM = 128
N_KV_HEADS = 4
KV_GROUP = N_HEADS // N_KV_HEADS
ATTN_WIDTH = N_HEADS * HEAD_DIM
KV_WIDTH = N_KV_HEADS * HEAD_DIM
MOBA_BLOCK = 256
MOBA_TOPK = 3
D_INNER = 4096
SSD_HEAD_DIM = 64
SSD_HEADS = D_INNER // SSD_HEAD_DIM
SSD_GROUPS = 8
GROUP_WIDTH = D_INNER // SSD_GROUPS
D_STATE = 128
CONV_WIDTH = 4
CONV_DIM = D_INNER + 2 * SSD_GROUPS * D_STATE
SSD_CHUNK = 128
D_FF = 5632
EPS = 1e-6
LANES = 128
SUBLANES = 8
NEG_BIG = -1e30

VMEM_LIMIT = 48 * 1024 * 1024


def _params(*sem):
    return pltpu.CompilerParams(dimension_semantics=sem, vmem_limit_bytes=VMEM_LIMIT)


def _split_hi_lo(x):
    hi = x.astype(BF16)
    lo = (x - hi.astype(F32)).astype(BF16)
    return hi, lo


def _silu(x):
    return x / (1.0 + jnp.exp(-x))


def _softplus(x):
    return jnp.maximum(x, 0.0) + jnp.log1p(jnp.exp(-jnp.abs(x)))


def _rmsnorm_kernel(x_ref, g_ref, o_ref):
    x = x_ref[...]
    y = x * lax.rsqrt(jnp.mean(x * x, axis=-1, keepdims=True) + EPS)
    o_ref[...] = (y * g_ref[...]).astype(o_ref.dtype)


def _rmsnorm(x, g, out_dtype=BF16, tm=512):
    m, d = x.shape
    tm = min(tm, m)
    return pl.pallas_call(
        _rmsnorm_kernel,
        grid=(pl.cdiv(m, tm),),
        in_specs=[pl.BlockSpec((tm, d), lambda i: (i, 0)), pl.BlockSpec((1, d), lambda i: (0, 0))],
        out_specs=pl.BlockSpec((tm, d), lambda i: (i, 0)),
        out_shape=jax.ShapeDtypeStruct((m, d), out_dtype),
        compiler_params=_params("parallel"),
        name="rmsnorm",
    )(x, g.reshape(1, d))


def _matmul_kernel(a_ref, w_ref, o_ref):
    o_ref[...] = jnp.dot(a_ref[...], w_ref[...], preferred_element_type=F32).astype(o_ref.dtype)


def _matmul(a, w, out_dtype=F32, tm=1024, tn=512, name="matmul"):
    m, k = a.shape
    n = w.shape[1]
    tm, tn = min(tm, m), min(tn, n)
    return pl.pallas_call(
        _matmul_kernel,
        grid=(pl.cdiv(m, tm), n // tn),
        in_specs=[pl.BlockSpec((tm, k), lambda i, j: (i, 0)), pl.BlockSpec((k, tn), lambda i, j: (0, j))],
        out_specs=pl.BlockSpec((tm, tn), lambda i, j: (i, j)),
        out_shape=jax.ShapeDtypeStruct((m, n), out_dtype),
        compiler_params=_params("parallel", "arbitrary"),
        name=name,
    )(a, w)


def _mix_kernel(att_ref, ssd_ref, wa_ref, ws_ref, ga_ref, gs_ref, o_ref):
    ya = jnp.dot(att_ref[...], wa_ref[...], preferred_element_type=F32)
    ys = jnp.dot(ssd_ref[...], ws_ref[...], preferred_element_type=F32)
    o_ref[...] = (jax.nn.sigmoid(ga_ref[...]) * ya + jax.nn.sigmoid(gs_ref[...]) * ys).astype(o_ref.dtype)


def _mix(att, ssd, wa, ws, gates, tm=1024, tn=256):
    m = att.shape[0]
    tm = min(tm, m)
    nj = D_MODEL // tn
    return pl.pallas_call(
        _mix_kernel,
        grid=(pl.cdiv(m, tm), nj),
        in_specs=[
            pl.BlockSpec((tm, ATTN_WIDTH), lambda i, j: (i, 0)),
            pl.BlockSpec((tm, D_INNER), lambda i, j: (i, 0)),
            pl.BlockSpec((ATTN_WIDTH, tn), lambda i, j: (0, j)),
            pl.BlockSpec((D_INNER, tn), lambda i, j: (0, j)),
            pl.BlockSpec((tm, tn), lambda i, j: (i, j)),
            pl.BlockSpec((tm, tn), lambda i, j: (i, j + nj)),
        ],
        out_specs=pl.BlockSpec((tm, tn), lambda i, j: (i, j)),
        out_shape=jax.ShapeDtypeStruct((m, D_MODEL), BF16),
        compiler_params=_params("parallel", "arbitrary"),
        name="mix",
    )(att, ssd, wa, ws, gates, gates)


def _oproj_kernel(mix_ref, wo_ref, x_ref, g_ref, x1_ref, hn_ref):
    x1 = x_ref[...] + jnp.dot(mix_ref[...], wo_ref[...], preferred_element_type=F32)
    x1_ref[...] = x1
    y = x1 * lax.rsqrt(jnp.mean(x1 * x1, axis=-1, keepdims=True) + EPS)
    hn_ref[...] = (y * g_ref[...]).astype(hn_ref.dtype)


def _oproj(mixed, wo, x, g, tm=512):
    m = x.shape[0]
    tm = min(tm, m)
    return pl.pallas_call(
        _oproj_kernel,
        grid=(pl.cdiv(m, tm),),
        in_specs=[
            pl.BlockSpec((tm, D_MODEL), lambda i: (i, 0)),
            pl.BlockSpec((D_MODEL, D_MODEL), lambda i: (0, 0)),
            pl.BlockSpec((tm, D_MODEL), lambda i: (i, 0)),
            pl.BlockSpec((1, D_MODEL), lambda i: (0, 0)),
        ],
        out_specs=[pl.BlockSpec((tm, D_MODEL), lambda i: (i, 0)), pl.BlockSpec((tm, D_MODEL), lambda i: (i, 0))],
        out_shape=[jax.ShapeDtypeStruct((m, D_MODEL), F32), jax.ShapeDtypeStruct((m, D_MODEL), BF16)],
        compiler_params=_params("parallel"),
        name="oproj",
    )(mixed, wo, x, g.reshape(1, D_MODEL))


def _ffn_in_kernel(h_ref, wg_ref, wu_ref, o_ref):
    h = h_ref[...]
    gate = jnp.dot(h, wg_ref[...], preferred_element_type=F32)
    up = jnp.dot(h, wu_ref[...], preferred_element_type=F32)
    o_ref[...] = (_silu(gate) * up).astype(o_ref.dtype)


def _ffn_in(hn, w, tm=1024, tn=512):
    m = hn.shape[0]
    tm = min(tm, m)
    nj = D_FF // tn
    return pl.pallas_call(
        _ffn_in_kernel,
        grid=(pl.cdiv(m, tm), nj),
        in_specs=[
            pl.BlockSpec((tm, D_MODEL), lambda i, j: (i, 0)),
            pl.BlockSpec((D_MODEL, tn), lambda i, j: (0, j)),
            pl.BlockSpec((D_MODEL, tn), lambda i, j: (0, j + nj)),
        ],
        out_specs=pl.BlockSpec((tm, tn), lambda i, j: (i, j)),
        out_shape=jax.ShapeDtypeStruct((m, D_FF), BF16),
        compiler_params=_params("parallel", "arbitrary"),
        name="ffn_in",
    )(hn, w, w)


def _ffn_out_kernel(h_ref, w_ref, x_ref, g_ref, o_ref):
    k = pl.program_id(1)

    @pl.when(k == 0)
    def _():
        o_ref[...] = x_ref[...]

    o_ref[...] += jnp.dot(h_ref[...], w_ref[...], preferred_element_type=F32)

    @pl.when(k == pl.num_programs(1) - 1)
    def _():
        x2 = o_ref[...]
        y = x2 * lax.rsqrt(jnp.mean(x2 * x2, axis=-1, keepdims=True) + EPS)
        o_ref[...] = y * g_ref[...]


def _ffn_out(h, w, x1, g, tm=1024, tk=512):
    m = h.shape[0]
    tm = min(tm, m)
    return pl.pallas_call(
        _ffn_out_kernel,
        grid=(pl.cdiv(m, tm), D_FF // tk),
        in_specs=[
            pl.BlockSpec((tm, tk), lambda i, k: (i, k)),
            pl.BlockSpec((tk, D_MODEL), lambda i, k: (k, 0)),
            pl.BlockSpec((tm, D_MODEL), lambda i, k: (i, 0)),
            pl.BlockSpec((1, D_MODEL), lambda i, k: (0, 0)),
        ],
        out_specs=pl.BlockSpec((tm, D_MODEL), lambda i, k: (i, 0)),
        out_shape=jax.ShapeDtypeStruct((m, D_MODEL), F32),
        compiler_params=_params("parallel", "arbitrary"),
        name="ffn_out",
    )(h, w, x1, g.reshape(1, D_MODEL))


def _kmean_kernel(k_ref, o_ref):
    t = k_ref.shape[0]
    nb = t // MOBA_BLOCK
    km = jnp.mean(k_ref[...].reshape(nb, MOBA_BLOCK, KV_WIDTH), axis=1)
    o_ref[0] = jnp.concatenate([km, jnp.zeros((LANES - nb, KV_WIDTH), F32)], axis=0)


def _kmean(k, batch):
    t = k.shape[0] // batch
    return pl.pallas_call(
        _kmean_kernel,
        grid=(batch,),
        in_specs=[pl.BlockSpec((t, KV_WIDTH), lambda b: (b, 0))],
        out_specs=pl.BlockSpec((1, LANES, KV_WIDTH), lambda b: (b, 0, 0)),
        out_shape=jax.ShapeDtypeStruct((batch, LANES, KV_WIDTH), F32),
        compiler_params=_params("parallel"),
        name="kmean",
    )(k)


def _top_rows(gate, row_id, n_valid, n_pick):
    picks = []
    for t in range(n_pick):
        mx = jnp.max(gate, axis=0, keepdims=True)
        idx = jnp.min(jnp.where(gate == mx, row_id, jnp.iinfo(jnp.int32).max), axis=0, keepdims=True)
        picks.append(jnp.where(t < n_valid, idx, -1))
        gate = jnp.where(row_id == idx, -jnp.inf, gate)
    return picks


V_ROWS = HEAD_DIM + 16
LOG2E = 1.4426950408889634


def _moba_kernel(slopes_ref, q_ref, k_ref, v_ref, km_ref, o_ref,
                 kb_ref, vt_ref, bias_ref, ua_ref, ub_ref, m_ref, acc_ref):
    kvh = pl.program_id(1)
    i = pl.program_id(2)
    nq = KV_GROUP * MOBA_BLOCK
    nb = vt_ref.shape[0]
    nb_rows = -(-nb // 8) * 8

    @pl.when(i == 0)
    def _():
        kb_ref[...] = k_ref[...].astype(BF16)
        for n in range(nb):
            vt_ref[n, 0:HEAD_DIM, :] = v_ref[n * MOBA_BLOCK:(n + 1) * MOBA_BLOCK, :].T.astype(BF16)
            vt_ref[n, HEAD_DIM:V_ROWS, :] = jnp.ones((V_ROWS - HEAD_DIM, MOBA_BLOCK), BF16)

    q = q_ref[...]
    qs = jnp.concatenate([q[:, g * HEAD_DIM:(g + 1) * HEAD_DIM] for g in range(KV_GROUP)], axis=0)
    qsb = (qs * (HEAD_DIM ** -0.5 * LOG2E)).astype(BF16)

    contract_last = (((1,), (1,)), ((), ()))
    q_hi, q_lo = _split_hi_lo(qs)
    km_hi, km_lo = _split_hi_lo(km_ref[0, 0:nb_rows, :])
    gate = (lax.dot_general(km_hi, q_hi, contract_last, preferred_element_type=F32)
            + lax.dot_general(km_hi, q_lo, contract_last, preferred_element_type=F32)
            + lax.dot_general(km_lo, q_hi, contract_last, preferred_element_type=F32))
    blk = lax.broadcasted_iota(jnp.int32, (nb_rows, nq), 0)
    picks = _top_rows(jnp.where(blk < i, gate, -jnp.inf), blk, i, MOBA_TOPK)

    qlane = lax.broadcasted_iota(jnp.int32, (1, nq), 1)
    slope = jnp.zeros((1, nq), F32)
    for g in range(KV_GROUP):
        slope = jnp.where(qlane // MOBA_BLOCK == g, slopes_ref[kvh * KV_GROUP + g] * LOG2E, slope)
    kidx = lax.broadcasted_iota(jnp.int32, (MOBA_BLOCK, nq), 0)
    bias_ref[...] = kidx.astype(F32) * slope

    m_ref[...] = jnp.full((1, nq), NEG_BIG, F32)
    acc_ref[...] = jnp.zeros((V_ROWS, nq), F32)

    def scores(n):
        start = pl.multiple_of(n * MOBA_BLOCK, MOBA_BLOCK)
        kb = kb_ref[pl.ds(start, MOBA_BLOCK), :]
        return lax.dot_general(kb, qsb, contract_last, preferred_element_type=F32) + bias_ref[...]

    def past_update(u_ref, n):
        u = u_ref[...]
        c = ((i - n) * MOBA_BLOCK).astype(F32) * slope
        sel = (picks[0] == n) | (picks[1] == n) | (picks[2] == n)
        m_old = m_ref[...]
        m_new = jnp.where(sel, jnp.maximum(m_old, jnp.max(u, axis=0, keepdims=True) - c), m_old)
        p = jnp.exp2(u - (jnp.where(sel, m_new, -NEG_BIG) + c))
        alpha = jnp.exp2(m_old - m_new)
        acc_ref[...] = alpha * acc_ref[...] + jnp.dot(vt_ref[n], p.astype(BF16), preferred_element_type=F32)
        m_ref[...] = m_new

    def own_update(u_ref):
        u = jnp.where(kidx <= qlane % MOBA_BLOCK, u_ref[...], NEG_BIG)
        m_old = m_ref[...]
        m_new = jnp.maximum(m_old, jnp.max(u, axis=0, keepdims=True))
        p = jnp.exp2(u - m_new)
        alpha = jnp.exp2(m_old - m_new)
        acc = alpha * acc_ref[...] + jnp.dot(vt_ref[i], p.astype(BF16), preferred_element_type=F32)
        out = acc[0:HEAD_DIM, :] / acc[HEAD_DIM:HEAD_DIM + 1, :]
        for g in range(KV_GROUP):
            o_ref[:, g * HEAD_DIM:(g + 1) * HEAD_DIM] = (
                out[:, g * MOBA_BLOCK:(g + 1) * MOBA_BLOCK].T.astype(o_ref.dtype))

    ua_ref[...] = scores(0)

    def two_blocks(k, carry):
        n = 2 * k
        ub_ref[...] = scores(n + 1)
        past_update(ua_ref, n)
        ua_ref[...] = scores(n + 2)
        past_update(ub_ref, n + 1)
        return carry

    lax.fori_loop(0, i // 2, two_blocks, 0)

    @pl.when(i % 2 == 1)
    def _():
        ub_ref[...] = scores(i)
        past_update(ua_ref, i - 1)
        own_update(ub_ref)

    @pl.when(i % 2 == 0)
    def _():
        own_update(ua_ref)


def _moba_prompt(q, k, v, kmean, slopes, batch):
    t = q.shape[0] // batch
    nb = t // MOBA_BLOCK
    assert nb <= LANES
    nq = KV_GROUP * MOBA_BLOCK
    grid_spec = pltpu.PrefetchScalarGridSpec(
        num_scalar_prefetch=1,
        grid=(batch, N_KV_HEADS, nb),
        in_specs=[
            pl.BlockSpec((MOBA_BLOCK, KV_GROUP * HEAD_DIM), lambda b, h, i, s: (b * nb + i, h)),
            pl.BlockSpec((t, HEAD_DIM), lambda b, h, i, s: (b, h)),
            pl.BlockSpec((t, HEAD_DIM), lambda b, h, i, s: (b, h)),
            pl.BlockSpec((1, LANES, HEAD_DIM), lambda b, h, i, s: (b, 0, h)),
        ],
        out_specs=pl.BlockSpec((MOBA_BLOCK, KV_GROUP * HEAD_DIM), lambda b, h, i, s: (b * nb + i, h)),
        scratch_shapes=[
            pltpu.VMEM((t, HEAD_DIM), BF16),
            pltpu.VMEM((nb, V_ROWS, MOBA_BLOCK), BF16),
            pltpu.VMEM((MOBA_BLOCK, nq), F32),
            pltpu.VMEM((MOBA_BLOCK, nq), F32),
            pltpu.VMEM((MOBA_BLOCK, nq), F32),
            pltpu.VMEM((1, nq), F32),
            pltpu.VMEM((V_ROWS, nq), F32),
        ],
    )
    return pl.pallas_call(
        _moba_kernel,
        grid_spec=grid_spec,
        out_shape=jax.ShapeDtypeStruct((batch * t, ATTN_WIDTH), BF16),
        compiler_params=_params("arbitrary", "arbitrary", "arbitrary"),
        name="moba_prompt",
    )(slopes, q, k, v, kmean)


def _ssd_prompt_kernel(xbc_ref, z_ref, dt_ref, cw_ref, cb_ref, dtb_ref, alog_ref, dsk_ref, nrm_ref, e_ref,
                       y_ref, st_ref, xpad_ref):
    c = pl.program_id(1)
    L = SSD_CHUNK
    head = 8

    @pl.when(c == 0)
    def _():
        st_ref[...] = jnp.zeros_like(st_ref)
        xpad_ref[0:head, :] = jnp.zeros((head, CONV_DIM), F32)

    xpad_ref[head:head + L, :] = xbc_ref[...]
    off = head - (CONV_WIDTH - 1)
    conv = cb_ref[...] + xpad_ref[off:off + L, :] * cw_ref[0:1, :]
    for i in range(1, CONV_WIDTH):
        conv = conv + xpad_ref[off + i:off + i + L, :] * cw_ref[i:i + 1, :]
    xpad_ref[0:head, :] = xpad_ref[L:L + head, :]
    act = _silu(conv)
    xs = act[:, :D_INNER]
    bm = act[:, D_INNER:D_INNER + SSD_GROUPS * D_STATE]
    cm = act[:, D_INNER + SSD_GROUPS * D_STATE:]

    dt = _softplus(dt_ref[...] + dtb_ref[...])
    la = dt * (-jnp.exp(alog_ref[...]))
    r_i = lax.broadcasted_iota(jnp.int32, (L, L), 0)
    c_i = lax.broadcasted_iota(jnp.int32, (L, L), 1)
    causal = r_i >= c_i
    tri = causal.astype(BF16)
    la_hi = la.astype(BF16)
    la_r = la - la_hi.astype(F32)
    la_mid = la_r.astype(BF16)
    la_lo = (la_r - la_mid.astype(F32)).astype(BF16)
    acs = (jnp.dot(tri, la_hi, preferred_element_type=F32) + jnp.dot(tri, la_mid, preferred_element_type=F32)
           + jnp.dot(tri, la_lo, preferred_element_type=F32))
    acs_t = acs.T
    dt_t = dt.T
    acs_last = acs[L - 1:L, :]
    e = e_ref[...]
    exp_acs_x = jnp.dot(jnp.exp(acs).astype(BF16), e, preferred_element_type=F32)
    wend_x = jnp.dot((dt * jnp.exp(acs_last - acs)).astype(BF16), e, preferred_element_type=F32)
    dec_last = jnp.exp(acs_t[:, L - 1:L])
    lane = lax.broadcasted_iota(jnp.int32, (L, LANES), 1)
    contract_last = (((1,), (1,)), ((), ()))
    contract_first = (((0,), (0,)), ((), ()))
    z = z_ref[...]

    for g in range(SSD_GROUPS):
        gs = slice(g * GROUP_WIDTH, (g + 1) * GROUP_WIDTH)
        bg = bm[:, g * D_STATE:(g + 1) * D_STATE].astype(BF16)
        cg = cm[:, g * D_STATE:(g + 1) * D_STATE].astype(BF16)
        cb = lax.dot_general(cg, bg, contract_last, preferred_element_type=F32)
        st = st_ref[0, gs, :]
        y_state = lax.dot_general(cg, st.astype(BF16), contract_last, preferred_element_type=F32)
        pairs = []
        for j in range(GROUP_WIDTH // LANES):
            col = g * (GROUP_WIDTH // LANES) + j
            ms = []
            for h in (2 * col, 2 * col + 1):
                seg = acs[:, h:h + 1] - acs_t[h:h + 1, :]
                ms.append(cb * jnp.exp(jnp.where(causal, seg, -jnp.inf)) * dt_t[h:h + 1, :])
            xcol = xs[:, col * LANES:(col + 1) * LANES]
            rhs = jnp.concatenate([jnp.where(lane < SSD_HEAD_DIM, xcol, 0.0),
                                   jnp.where(lane >= SSD_HEAD_DIM, xcol, 0.0)], axis=0).astype(BF16)
            lhs = jnp.concatenate(ms, axis=1).astype(BF16)
            pairs.append(jnp.dot(lhs, rhs, preferred_element_type=F32))
        xg = xs[:, gs]
        yg = jnp.concatenate(pairs, axis=1) + y_state * exp_acs_x[:, gs] + dsk_ref[:, gs] * xg
        yg = yg * _silu(z[:, gs])
        yg = yg * lax.rsqrt(jnp.mean(yg * yg, axis=-1, keepdims=True) + EPS)
        y_ref[:, gs] = (yg * nrm_ref[:, gs]).astype(y_ref.dtype)

        upd = lax.dot_general((xg * wend_x[:, gs]).astype(BF16), bg, contract_first, preferred_element_type=F32)
        drows = jnp.concatenate(
            [jnp.broadcast_to(dec_last[h:h + 1, :], (SSD_HEAD_DIM, D_STATE))
             for h in range(g * 8, (g + 1) * 8)], axis=0)
        st_ref[0, gs, :] = st * drows + upd


def _ssd_prompt(xbc, z, dt_raw, cw, cb, dtb, alog, dsk_x, nrm, e, batch):
    t = xbc.shape[0] // batch
    nc = t // SSD_CHUNK
    row = lambda b, c: (b * nc + c, 0)
    const = lambda b, c: (0, 0)
    return pl.pallas_call(
        _ssd_prompt_kernel,
        grid=(batch, nc),
        in_specs=[
            pl.BlockSpec((SSD_CHUNK, CONV_DIM), row),
            pl.BlockSpec((SSD_CHUNK, D_INNER), row),
            pl.BlockSpec((SSD_CHUNK, LANES), row),
            pl.BlockSpec((CONV_WIDTH, CONV_DIM), const),
            pl.BlockSpec((1, CONV_DIM), const),
            pl.BlockSpec((1, LANES), const),
            pl.BlockSpec((1, LANES), const),
            pl.BlockSpec((1, D_INNER), const),
            pl.BlockSpec((1, D_INNER), const),
            pl.BlockSpec((LANES, D_INNER), const),
        ],
        out_specs=[
            pl.BlockSpec((SSD_CHUNK, D_INNER), row),
            pl.BlockSpec((1, D_INNER, D_STATE), lambda b, c: (b, 0, 0)),
        ],
        out_shape=[
            jax.ShapeDtypeStruct((batch * t, D_INNER), BF16),
            jax.ShapeDtypeStruct((batch, D_INNER, D_STATE), F32),
        ],
        scratch_shapes=[pltpu.VMEM((SSD_CHUNK + 8, CONV_DIM), F32)],
        compiler_params=_params("arbitrary", "arbitrary"),
        name="ssd_prompt",
    )(xbc, z, dt_raw, cw, cb, dtb, alog, dsk_x, nrm, e)


KMEAN_PAGES_PER_STEP = 16


def _dec_kmean_kernel(pt_ref, *refs):
    k_refs, o_ref = refs[:-1], refs[-1]
    rows = k_refs[0].shape[1]
    pages_per_block = MOBA_BLOCK * N_KV_HEADS // rows
    for j in range(len(k_refs) // pages_per_block):
        acc = jnp.zeros((SUBLANES, HEAD_DIM), F32)
        for page in k_refs[j * pages_per_block:(j + 1) * pages_per_block]:
            acc = acc + jnp.sum(page[0].reshape(rows // SUBLANES, SUBLANES, HEAD_DIM), axis=0)
        km = (acc[0:N_KV_HEADS] + acc[N_KV_HEADS:SUBLANES]) * (1.0 / MOBA_BLOCK)
        for c in range(N_KV_HEADS):
            o_ref[0, c, j:j + 1, :] = km[c:c + 1, :]


def _dec_kmean(kc, page_table, nb):
    nseq, n_pages = page_table.shape
    pps = min(KMEAN_PAGES_PER_STEP, n_pages)
    rows = kc.shape[1]
    bps = pps * rows // (MOBA_BLOCK * N_KV_HEADS)
    page_spec = lambda j: pl.BlockSpec((1, rows, HEAD_DIM), lambda b, s, pt: (pt[b, s * pps + j], 0, 0))
    grid_spec = pltpu.PrefetchScalarGridSpec(
        num_scalar_prefetch=1,
        grid=(nseq, n_pages // pps),
        in_specs=[page_spec(j) for j in range(pps)],
        out_specs=pl.BlockSpec((1, N_KV_HEADS, bps, HEAD_DIM), lambda b, s, pt: (b, 0, s, 0)),
    )
    return pl.pallas_call(
        _dec_kmean_kernel,
        grid_spec=grid_spec,
        out_shape=jax.ShapeDtypeStruct((nseq, N_KV_HEADS, nb, HEAD_DIM), F32),
        compiler_params=_params("arbitrary", "arbitrary"),
        name="dec_kmean",
    )(page_table, *([kc] * pps))


def _dec_select_kernel(q_ref, km_ref, sel_ref):
    nb = km_ref.shape[2]
    hrow = lax.broadcasted_iota(jnp.int32, (N_HEADS, 1), 0)
    contract_last = (((1,), (1,)), ((), ()))
    q_hi, q_lo = _split_hi_lo(q_ref[0])
    gate = jnp.zeros((N_HEADS, nb), F32)
    for c in range(N_KV_HEADS):
        km_hi, km_lo = _split_hi_lo(km_ref[0, c])
        g = (lax.dot_general(q_hi, km_hi, contract_last, preferred_element_type=F32)
             + lax.dot_general(q_lo, km_hi, contract_last, preferred_element_type=F32)
             + lax.dot_general(q_hi, km_lo, contract_last, preferred_element_type=F32))
        gate = jnp.where(hrow // KV_GROUP == c, g, gate)
    blk = lax.broadcasted_iota(jnp.int32, (N_HEADS, nb), 1)
    lane = lax.broadcasted_iota(jnp.int32, (N_HEADS, LANES), 1)
    sel = jnp.zeros((N_HEADS, LANES), jnp.int32)
    for t in range(MOBA_TOPK):
        mx = jnp.max(gate, axis=-1, keepdims=True)
        idx = jnp.min(jnp.where(gate == mx, blk, nb), axis=-1, keepdims=True)
        sel = jnp.where(lane == t, idx, sel)
        gate = jnp.where(blk == idx, -jnp.inf, gate)
    sel_ref[0] = sel


def _dec_select(q, kmean):
    nseq, _, nb, _ = kmean.shape
    assert nb >= MOBA_TOPK
    return pl.pallas_call(
        _dec_select_kernel,
        grid=(nseq,),
        in_specs=[pl.BlockSpec((1, N_HEADS, HEAD_DIM), lambda b: (b, 0, 0)),
                  pl.BlockSpec((1, N_KV_HEADS, nb, HEAD_DIM), lambda b: (b, 0, 0, 0))],
        out_specs=pl.BlockSpec((1, N_HEADS, LANES), lambda b: (b, 0, 0)),
        out_shape=jax.ShapeDtypeStruct((nseq, N_HEADS, LANES), jnp.int32),
        compiler_params=_params("parallel"),
        name="dec_select",
    )(q.reshape(nseq, N_HEADS, HEAD_DIM), kmean)


def _dec_attn_kernel(pt_ref, sel_ref, slopes_ref, q_ref, kn_ref, vn_ref, *refs, past_len, pages_per_block):
    n_sel = MOBA_TOPK * pages_per_block
    k_refs, v_refs, o_ref = refs[:n_sel], refs[n_sel:2 * n_sel], refs[2 * n_sel]
    b = pl.program_id(0)
    h = pl.program_id(1)
    c = h // KV_GROUP
    rows = k_refs[0].shape[1]
    keys_per_page = rows // N_KV_HEADS
    scale = HEAD_DIM ** -0.5
    qh = q_ref[0, pl.ds(h, 1), :]
    slope = slopes_ref[h]
    r = lax.broadcasted_iota(jnp.int32, (rows, 1), 0)
    mine = r % N_KV_HEADS == c
    scores = []
    for t in range(MOBA_TOPK):
        blk = sel_ref[b, h * MOBA_TOPK + t]
        for j in range(pages_per_block):
            s = jnp.sum(k_refs[t * pages_per_block + j][0] * qh, axis=-1, keepdims=True) * scale
            kpos = blk * MOBA_BLOCK + j * keys_per_page + r // N_KV_HEADS
            s = s - slope * (past_len - kpos).astype(F32)
            scores.append(jnp.where(mine, s, NEG_BIG))
    kn = kn_ref[0, pl.ds(c, 1), :]
    vn = vn_ref[0, pl.ds(c, 1), :]
    s_self = jnp.sum(qh * kn, axis=-1, keepdims=True) * scale
    m = s_self
    for s in scores:
        m = jnp.maximum(m, jnp.max(s, axis=0, keepdims=True))
    w_self = jnp.exp(s_self - m)
    l = w_self
    o = w_self * vn
    for s, v_ref in zip(scores, v_refs):
        p = jnp.exp(s - m)
        l = l + jnp.sum(p, axis=0, keepdims=True)
        o = o + jnp.sum(p * v_ref[0], axis=0, keepdims=True)
    o_ref[0, 0] = o / l


def _dec_attn(q, k_new, v_new, kc, vc, page_table, sel, slopes):
    nseq, n_pages = page_table.shape
    rows = kc.shape[1]
    ppb = MOBA_BLOCK * N_KV_HEADS // rows
    page_spec = lambda t, j: pl.BlockSpec(
        (1, rows, HEAD_DIM), lambda b, h, pt, sel, sl: (pt[b, ppb * sel[b, h * MOBA_TOPK + t] + j], 0, 0))
    page_specs = [page_spec(t, j) for t in range(MOBA_TOPK) for j in range(ppb)]
    kv_spec = pl.BlockSpec((1, N_KV_HEADS, HEAD_DIM), lambda b, h, pt, sel, sl: (b, 0, 0))
    grid_spec = pltpu.PrefetchScalarGridSpec(
        num_scalar_prefetch=3,
        grid=(nseq, N_HEADS),
        in_specs=[pl.BlockSpec((1, N_HEADS, HEAD_DIM), lambda b, h, pt, sel, sl: (b, 0, 0)), kv_spec, kv_spec]
        + page_specs + page_specs,
        out_specs=pl.BlockSpec((1, 1, 1, HEAD_DIM), lambda b, h, pt, sel, sl: (b, h, 0, 0)),
    )
    n_sel = len(page_specs)
    out = pl.pallas_call(
        functools.partial(_dec_attn_kernel, past_len=n_pages * rows // N_KV_HEADS, pages_per_block=ppb),
        grid_spec=grid_spec,
        out_shape=jax.ShapeDtypeStruct((nseq, N_HEADS, 1, HEAD_DIM), F32),
        compiler_params=_params("arbitrary", "arbitrary"),
        name="dec_attn",
    )(page_table, sel, slopes, q.reshape(nseq, N_HEADS, HEAD_DIM), k_new.reshape(nseq, N_KV_HEADS, HEAD_DIM),
      v_new.reshape(nseq, N_KV_HEADS, HEAD_DIM), *([kc] * n_sel), *([vc] * n_sel))
    return out.reshape(nseq, ATTN_WIDTH).astype(BF16)


def _ssd_step_prep_kernel(xbc_ref, cs_ref, dt_ref, cw_ref, cb_ref, dtb_ref, alog_ref, e_ref,
                          xs_ref, b_ref, c_ref, xd_ref, da_ref, cn_ref):
    x = xbc_ref[...]
    conv = cb_ref[...] + cs_ref[0] * cw_ref[0:1, :]
    for i in range(1, CONV_WIDTH - 1):
        conv = conv + cs_ref[i] * cw_ref[i:i + 1, :]
    conv = conv + x * cw_ref[CONV_WIDTH - 1:CONV_WIDTH, :]
    for i in range(CONV_WIDTH - 2):
        cn_ref[i] = cs_ref[i + 1]
    cn_ref[CONV_WIDTH - 2] = x
    act = _silu(conv)
    xs = act[:, :D_INNER]
    xs_ref[...] = xs
    b_ref[...] = act[:, D_INNER:D_INNER + SSD_GROUPS * D_STATE]
    c_ref[...] = act[:, D_INNER + SSD_GROUPS * D_STATE:]
    dt = _softplus(dt_ref[...] + dtb_ref[...])
    da = jnp.exp(dt * (-jnp.exp(alog_ref[...])))
    e = e_ref[...]
    dt_hi, dt_lo = _split_hi_lo(dt)
    da_hi, da_lo = _split_hi_lo(da)
    dt_x = jnp.dot(dt_hi, e, preferred_element_type=F32) + jnp.dot(dt_lo, e, preferred_element_type=F32)
    da_ref[...] = jnp.dot(da_hi, e, preferred_element_type=F32) + jnp.dot(da_lo, e, preferred_element_type=F32)
    xd_ref[...] = xs * dt_x


def _ssd_step_prep(xbc, conv_state_t, dt_raw, cw, cb, dtb, alog, e):
    n = xbc.shape[0]
    sds = lambda *shape: jax.ShapeDtypeStruct(shape, F32)
    return pl.pallas_call(
        _ssd_step_prep_kernel,
        out_shape=[sds(n, D_INNER), sds(n, SSD_GROUPS * D_STATE), sds(n, SSD_GROUPS * D_STATE),
                   sds(n, D_INNER), sds(n, D_INNER), sds(CONV_WIDTH - 1, n, CONV_DIM)],
        compiler_params=pltpu.CompilerParams(vmem_limit_bytes=VMEM_LIMIT),
        name="ssd_step_prep",
    )(xbc, conv_state_t, dt_raw, cw, cb, dtb, alog, e)


def _ssd_step_kernel(xd_ref, da_ref, xs_ref, z_ref, b_ref, c_ref, st_ref, dsk_ref, nrm_ref, y_ref, sn_ref):
    sub = 8
    row0 = lax.broadcasted_iota(jnp.int32, (sub, D_STATE), 0) == 0
    contract_last = (((1,), (1,)), ((), ()))
    contract_first = (((0,), (0,)), ((), ()))
    ones0 = jnp.where(row0, 1.0, 0.0).astype(BF16)
    for g in range(SSD_GROUPS):
        gs = slice(g * GROUP_WIDTH, (g + 1) * GROUP_WIDTH)
        ns = slice(g * D_STATE, (g + 1) * D_STATE)
        xd8 = jnp.broadcast_to(xd_ref[0, :, gs], (sub, GROUP_WIDTH))
        da8 = jnp.broadcast_to(da_ref[0, :, gs], (sub, GROUP_WIDTH))
        b8 = jnp.where(row0, jnp.broadcast_to(b_ref[0, :, ns], (sub, D_STATE)), 0.0)
        c8 = jnp.broadcast_to(c_ref[0, :, ns], (sub, D_STATE)).astype(BF16)
        xd_hi, xd_lo = _split_hi_lo(xd8)
        b_hi, b_lo = _split_hi_lo(b8)
        outer = lambda a, b: lax.dot_general(a, b, contract_first, preferred_element_type=F32)
        upd = outer(xd_hi, b_hi) + outer(xd_lo, b_hi) + outer(xd_hi, b_lo)
        da_hi = da8.astype(BF16)
        da_r = da8 - da_hi.astype(F32)
        da_mid = da_r.astype(BF16)
        da_lo = (da_r - da_mid.astype(F32)).astype(BF16)
        dec = outer(da_hi, ones0) + outer(da_mid, ones0) + outer(da_lo, ones0)
        st_new = st_ref[0, gs, :] * dec + upd
        sn_ref[0, gs, :] = st_new
        yg = lax.dot_general(c8, st_new.astype(BF16), contract_last, preferred_element_type=F32)[0:1, :]
        yg = yg + dsk_ref[:, gs] * xs_ref[0, :, gs]
        yg = yg * _silu(z_ref[0, :, gs])
        yg = yg * lax.rsqrt(jnp.mean(yg * yg, axis=-1, keepdims=True) + EPS)
        y_ref[0, :, gs] = (yg * nrm_ref[:, gs]).astype(y_ref.dtype)


def _ssd_step(xd, da, xs, z, bv, cv, state, dsk_x, nrm):
    n = xd.shape[0]
    r3 = lambda a: a.reshape(n, 1, a.shape[-1])
    wide = pl.BlockSpec((1, 1, D_INNER), lambda b: (b, 0, 0))
    narrow = pl.BlockSpec((1, 1, SSD_GROUPS * D_STATE), lambda b: (b, 0, 0))
    st_spec = pl.BlockSpec((1, D_INNER, D_STATE), lambda b: (b, 0, 0))
    const = pl.BlockSpec((1, D_INNER), lambda b: (0, 0))
    y, st_new = pl.pallas_call(
        _ssd_step_kernel,
        grid=(n,),
        in_specs=[wide, wide, wide, wide, narrow, narrow, st_spec, const, const],
        out_specs=[wide, st_spec],
        out_shape=[jax.ShapeDtypeStruct((n, 1, D_INNER), BF16), jax.ShapeDtypeStruct((n, D_INNER, D_STATE), F32)],
        compiler_params=_params("parallel"),
        name="ssd_step",
    )(r3(xd), r3(da), r3(xs), r3(z), r3(bv), r3(cv), state, dsk_x, nrm)
    return y.reshape(n, D_INNER), st_new


def _alibi_slopes():
    return jnp.exp2(-8.0 * jnp.arange(1, N_HEADS + 1, dtype=F32) / N_HEADS)


def _prep_weights(norm_mix, w_in, conv_w, conv_b, dt_bias, a_log, d_skip, ssd_norm, w_attn_out, w_ssd_out, w_o,
                  norm_ffn, w_ffn_in, w_ffn_out):
    sizes = (2 * D_MODEL, ATTN_WIDTH, KV_WIDTH, KV_WIDTH, D_INNER, CONV_DIM, SSD_HEADS)
    segs, start = [], 0
    for size in sizes:
        segs.append(w_in[:, start:start + size].astype(BF16))
        start += size
    segs[-1] = jnp.pad(segs[-1], ((0, 0), (0, LANES - SSD_HEADS)))
    pad_heads = lambda a: jnp.pad(a.astype(F32), (0, LANES - SSD_HEADS)).reshape(1, LANES)
    head_of_channel = jnp.arange(D_INNER, dtype=jnp.int32) // SSD_HEAD_DIM
    expand = (jnp.arange(LANES, dtype=jnp.int32)[:, None] == head_of_channel[None, :]).astype(BF16)
    return dict(
        norm_mix=norm_mix, w_segs=segs, conv_w=conv_w, conv_b=conv_b.reshape(1, CONV_DIM),
        dt_bias=pad_heads(dt_bias), a_log=pad_heads(a_log),
        d_skip_x=jnp.repeat(d_skip.astype(F32), SSD_HEAD_DIM).reshape(1, D_INNER),
        ssd_norm=ssd_norm.astype(F32).reshape(1, D_INNER), expand=expand,
        w_attn_out=w_attn_out.astype(BF16), w_ssd_out=w_ssd_out.astype(BF16), w_o=w_o.astype(BF16),
        norm_ffn=norm_ffn, w_ffn_in=w_ffn_in.astype(BF16), w_ffn_out=w_ffn_out.astype(BF16),
    )


def _project(x, w):
    hn = _rmsnorm(x, w["norm_mix"])
    names = ("gates", "q", "k", "v", "z", "xbc", "dt")
    return {name: _matmul(hn, seg, name="proj_" + name) for name, seg in zip(names, w["w_segs"])}


def _finish_layer(x, att, y_ssd, gates, w, norm_final):
    mixed = _mix(att, y_ssd, w["w_attn_out"], w["w_ssd_out"], gates)
    x1, hn = _oproj(mixed, w["w_o"], x, w["norm_ffn"])
    h = _ffn_in(hn, w["w_ffn_in"])
    return _ffn_out(h, w["w_ffn_out"], x1, norm_final)


def _prompt_layer(x, w, norm_final, slopes, batch):
    t = x.shape[0] // batch
    p = _project(x, w)
    kmean = _kmean(p["k"], batch)
    att = _moba_prompt(p["q"], p["k"], p["v"], kmean, slopes, batch)
    y_ssd, state = _ssd_prompt(p["xbc"], p["z"], p["dt"], w["conv_w"], w["conv_b"], w["dt_bias"], w["a_log"],
                               w["d_skip_x"], w["ssd_norm"], w["expand"], batch)
    y = _finish_layer(x, att, y_ssd, p["gates"], w, norm_final)
    conv_new = p["xbc"].reshape(batch, t, CONV_DIM)[:, t - (CONV_WIDTH - 1):, :]
    return y, p["k"], p["v"], conv_new, state


def _decode_layer(x, cache_k, cache_v, page_table, conv_state, ssm_state, w, norm_final, slopes):
    n = x.shape[0]
    p = _project(x, w)
    n_phys, page = cache_k.shape[0], cache_k.shape[1]
    kc = cache_k.reshape(n_phys, page * N_KV_HEADS, HEAD_DIM)
    vc = cache_v.reshape(n_phys, page * N_KV_HEADS, HEAD_DIM)
    nb = page_table.shape[1] * page // MOBA_BLOCK
    sel = _dec_select(p["q"], _dec_kmean(kc, page_table, nb))
    att = _dec_attn(p["q"], p["k"], p["v"], kc, vc, page_table,
                    sel[:, :, :MOBA_TOPK].reshape(n, N_HEADS * MOBA_TOPK), slopes)
    xs, bv, cv, xd, da, conv_new_t = _ssd_step_prep(
        p["xbc"], jnp.transpose(conv_state, (1, 0, 2)), p["dt"], w["conv_w"], w["conv_b"], w["dt_bias"],
        w["a_log"], w["expand"])
    y_ssd, state = _ssd_step(xd, da, xs, p["z"], bv, cv, ssm_state.reshape(n, D_INNER, D_STATE),
                             w["d_skip_x"], w["ssd_norm"])
    y = _finish_layer(x, att, y_ssd, p["gates"], w, norm_final)
    return y, p["k"], p["v"], jnp.transpose(conv_new_t, (1, 0, 2)), state


def kernel(x_prompt, x_sample, cache_k, cache_v, state_conv, state_ssm, page_table, norm_mix, w_in, conv_w, conv_b,
           dt_bias, a_log, d_skip, ssd_norm, w_attn_out, w_ssd_out, w_o, norm_ffn, w_ffn_in, w_ffn_out, norm_final):
    assert w_in.shape[0] == 1, "single-layer kernel"
    bp, t, _ = x_prompt.shape
    nd = x_sample.shape[0]
    assert x_sample.shape[1] == 1
    slopes = _alibi_slopes()
    w = _prep_weights(norm_mix[0], w_in[0], conv_w[0], conv_b[0], dt_bias[0], a_log[0], d_skip[0], ssd_norm[0],
                      w_attn_out[0], w_ssd_out[0], w_o[0], norm_ffn[0], w_ffn_in[0], w_ffn_out[0])

    yp, kp, vp, cp, hp = _prompt_layer(x_prompt.reshape(bp * t, D_MODEL), w, norm_final, slopes, bp)
    ys, ks, vs, cs, hs = _decode_layer(x_sample.reshape(nd, D_MODEL), cache_k[0], cache_v[0], page_table,
                                       state_conv[0], state_ssm[0], w, norm_final, slopes)
    kv = lambda a, b, s: a.reshape(1, b, s, N_KV_HEADS, HEAD_DIM)
    st = lambda a, b: a.reshape(1, b, SSD_HEADS, SSD_HEAD_DIM, D_STATE)
    return (yp.reshape(bp, t, D_MODEL), ys.reshape(nd, 1, D_MODEL),
            kv(kp, bp, t), kv(vp, bp, t), cp[None], st(hp, bp),
            kv(ks, nd, 1), kv(vs, nd, 1), cs[None], st(hs, nd))
```

```python
import functools

import jax
import jax.numpy as jnp
from jax import lax
from jax.experimental import pallas as pl
from jax.experimental.pallas import tpu as pltpu

F32 = jnp.float32
BF16 = jnp.bfloat16

D_MODEL = 2048
N_HEADS = 16
HEAD_DIM = 128
N_KV_HEADS = 4
KV_GROUP = N_HEADS // N_KV_HEADS
ATTN_WIDTH = N_HEADS * HEAD_DIM
KV_WIDTH = N_KV_HEADS * HEAD_DIM
MOBA_BLOCK = 256
MOBA_TOPK = 3
D_INNER = 4096
SSD_HEAD_DIM = 64
SSD_HEADS = D_INNER // SSD_HEAD_DIM
SSD_GROUPS = 8
GROUP_WIDTH = D_INNER // SSD_GROUPS
D_STATE = 128
CONV_WIDTH = 4
CONV_DIM = D_INNER + 2 * SSD_GROUPS * D_STATE
SSD_CHUNK = 128
D_FF = 5632
EPS = 1e-6
LANES = 128
SUBLANES = 8
NEG_BIG = -1e30

VMEM_LIMIT = 48 * 1024 * 1024


def _params(*sem):
    return pltpu.CompilerParams(dimension_semantics=sem, vmem_limit_bytes=VMEM_LIMIT)


def _split_hi_lo(x):
    hi = x.astype(BF16)
    lo = (x - hi.astype(F32)).astype(BF16)
    return hi, lo


def _silu(x):
    return x / (1.0 + jnp.exp(-x))


def _softplus(x):
    return jnp.maximum(x, 0.0) + jnp.log1p(jnp.exp(-jnp.abs(x)))


def _rmsnorm_kernel(x_ref, g_ref, o_ref):
    x = x_ref[...]
    y = x * lax.rsqrt(jnp.mean(x * x, axis=-1, keepdims=True) + EPS)
    o_ref[...] = (y * g_ref[...]).astype(o_ref.dtype)


def _rmsnorm(x, g, out_dtype=BF16, tm=512):
    m, d = x.shape
    tm = min(tm, m)
    return pl.pallas_call(
        _rmsnorm_kernel,
        grid=(pl.cdiv(m, tm),),
        in_specs=[pl.BlockSpec((tm, d), lambda i: (i, 0)), pl.BlockSpec((1, d), lambda i: (0, 0))],
        out_specs=pl.BlockSpec((tm, d), lambda i: (i, 0)),
        out_shape=jax.ShapeDtypeStruct((m, d), out_dtype),
        compiler_params=_params("parallel"),
        name="rmsnorm",
    )(x, g.reshape(1, d))


def _matmul_kernel(a_ref, w_ref, o_ref):
    o_ref[...] = jnp.dot(a_ref[...], w_ref[...], preferred_element_type=F32).astype(o_ref.dtype)


def _matmul(a, w, out_dtype=F32, tm=1024, tn=512, name="matmul"):
    m, k = a.shape
    n = w.shape[1]
    tm, tn = min(tm, m), min(tn, n)
    return pl.pallas_call(
        _matmul_kernel,
        grid=(pl.cdiv(m, tm), n // tn),
        in_specs=[pl.BlockSpec((tm, k), lambda i, j: (i, 0)), pl.BlockSpec((k, tn), lambda i, j: (0, j))],
        out_specs=pl.BlockSpec((tm, tn), lambda i, j: (i, j)),
        out_shape=jax.ShapeDtypeStruct((m, n), out_dtype),
        compiler_params=_params("parallel", "arbitrary"),
        name=name,
    )(a, w)


def _matmul_w32_kernel(a_ref, w_ref, o_ref, wb_ref):
    @pl.when(pl.program_id(1) == 0)
    def _():
        wb_ref[...] = w_ref[...].astype(BF16)

    o_ref[...] = jnp.dot(a_ref[...], wb_ref[...], preferred_element_type=F32).astype(o_ref.dtype)


def _matmul_w32(a, w, col0, n, out_dtype=F32, tm=1024, tn=1024, name="matmul"):
    m, k = a.shape
    tm, tn = min(tm, m), min(tn, n)
    assert col0 % tn == 0 and n % tn == 0
    return pl.pallas_call(
        _matmul_w32_kernel,
        grid=(n // tn, pl.cdiv(m, tm)),
        in_specs=[pl.BlockSpec((tm, k), lambda j, i: (i, 0)),
                  pl.BlockSpec((k, tn), lambda j, i: (0, col0 // tn + j))],
        out_specs=pl.BlockSpec((tm, tn), lambda j, i: (i, j)),
        out_shape=jax.ShapeDtypeStruct((m, n), out_dtype),
        scratch_shapes=[pltpu.VMEM((k, tn), BF16)],
        compiler_params=_params("arbitrary", "arbitrary"),
        name=name,
    )(a, w)


def _mix_kernel(att_ref, ssd_ref, wa_ref, ws_ref, ga_ref, gs_ref, o_ref):
    ya = jnp.dot(att_ref[...], wa_ref[...], preferred_element_type=F32)
    ys = jnp.dot(ssd_ref[...], ws_ref[...], preferred_element_type=F32)
    o_ref[...] = (jax.nn.sigmoid(ga_ref[...]) * ya + jax.nn.sigmoid(gs_ref[...]) * ys).astype(o_ref.dtype)


def _mix(att, ssd, wa, ws, gates, tm=1024, tn=256):
    m = att.shape[0]
    tm = min(tm, m)
    nj = D_MODEL // tn
    return pl.pallas_call(
        _mix_kernel,
        grid=(pl.cdiv(m, tm), nj),
        in_specs=[
            pl.BlockSpec((tm, ATTN_WIDTH), lambda i, j: (i, 0)),
            pl.BlockSpec((tm, D_INNER), lambda i, j: (i, 0)),
            pl.BlockSpec((ATTN_WIDTH, tn), lambda i, j: (0, j)),
            pl.BlockSpec((D_INNER, tn), lambda i, j: (0, j)),
            pl.BlockSpec((tm, tn), lambda i, j: (i, j)),
            pl.BlockSpec((tm, tn), lambda i, j: (i, j + nj)),
        ],
        out_specs=pl.BlockSpec((tm, tn), lambda i, j: (i, j)),
        out_shape=jax.ShapeDtypeStruct((m, D_MODEL), BF16),
        compiler_params=_params("parallel", "arbitrary"),
        name="mix",
    )(att, ssd, wa, ws, gates, gates)


def _oproj_kernel(mix_ref, wo_ref, x_ref, g_ref, x1_ref, hn_ref):
    x1 = x_ref[...] + jnp.dot(mix_ref[...], wo_ref[...], preferred_element_type=F32)
    x1_ref[...] = x1
    y = x1 * lax.rsqrt(jnp.mean(x1 * x1, axis=-1, keepdims=True) + EPS)
    hn_ref[...] = (y * g_ref[...]).astype(hn_ref.dtype)


def _oproj(mixed, wo, x, g, tm=512):
    m = x.shape[0]
    tm = min(tm, m)
    return pl.pallas_call(
        _oproj_kernel,
        grid=(pl.cdiv(m, tm),),
        in_specs=[
            pl.BlockSpec((tm, D_MODEL), lambda i: (i, 0)),
            pl.BlockSpec((D_MODEL, D_MODEL), lambda i: (0, 0)),
            pl.BlockSpec((tm, D_MODEL), lambda i: (i, 0)),
            pl.BlockSpec((1, D_MODEL), lambda i: (0, 0)),
        ],
        out_specs=[pl.BlockSpec((tm, D_MODEL), lambda i: (i, 0)), pl.BlockSpec((tm, D_MODEL), lambda i: (i, 0))],
        out_shape=[jax.ShapeDtypeStruct((m, D_MODEL), F32), jax.ShapeDtypeStruct((m, D_MODEL), BF16)],
        compiler_params=_params("parallel"),
        name="oproj",
    )(mixed, wo, x, g.reshape(1, D_MODEL))


def _ffn_in_kernel(h_ref, wg_ref, wu_ref, o_ref, wgb_ref, wub_ref):
    @pl.when(pl.program_id(1) == 0)
    def _():
        wgb_ref[...] = wg_ref[...].astype(BF16)
        wub_ref[...] = wu_ref[...].astype(BF16)

    h = h_ref[...]
    gate = jnp.dot(h, wgb_ref[...], preferred_element_type=F32)
    up = jnp.dot(h, wub_ref[...], preferred_element_type=F32)
    o_ref[...] = (_silu(gate) * up).astype(o_ref.dtype)


def _ffn_in(hn, w, tm=1024, tn=512):
    m = hn.shape[0]
    tm = min(tm, m)
    nj = D_FF // tn
    return pl.pallas_call(
        _ffn_in_kernel,
        grid=(nj, pl.cdiv(m, tm)),
        in_specs=[
            pl.BlockSpec((tm, D_MODEL), lambda j, i: (i, 0)),
            pl.BlockSpec((D_MODEL, tn), lambda j, i: (0, j)),
            pl.BlockSpec((D_MODEL, tn), lambda j, i: (0, j + nj)),
        ],
        out_specs=pl.BlockSpec((tm, tn), lambda j, i: (i, j)),
        out_shape=jax.ShapeDtypeStruct((m, D_FF), BF16),
        scratch_shapes=[pltpu.VMEM((D_MODEL, tn), BF16), pltpu.VMEM((D_MODEL, tn), BF16)],
        compiler_params=_params("arbitrary", "arbitrary"),
        name="ffn_in",
    )(hn, w, w)


def _ffn_out_kernel(h_ref, w_ref, x_ref, g_ref, o_ref):
    k = pl.program_id(1)

    @pl.when(k == 0)
    def _():
        o_ref[...] = x_ref[...]

    o_ref[...] += jnp.dot(h_ref[...], w_ref[...], preferred_element_type=F32)

    @pl.when(k == pl.num_programs(1) - 1)
    def _():
        x2 = o_ref[...]
        y = x2 * lax.rsqrt(jnp.mean(x2 * x2, axis=-1, keepdims=True) + EPS)
        o_ref[...] = y * g_ref[...]


def _ffn_out(h, w, x1, g, tm=1024, tk=512):
    m = h.shape[0]
    tm = min(tm, m)
    return pl.pallas_call(
        _ffn_out_kernel,
        grid=(pl.cdiv(m, tm), D_FF // tk),
        in_specs=[
            pl.BlockSpec((tm, tk), lambda i, k: (i, k)),
            pl.BlockSpec((tk, D_MODEL), lambda i, k: (k, 0)),
            pl.BlockSpec((tm, D_MODEL), lambda i, k: (i, 0)),
            pl.BlockSpec((1, D_MODEL), lambda i, k: (0, 0)),
        ],
        out_specs=pl.BlockSpec((tm, D_MODEL), lambda i, k: (i, 0)),
        out_shape=jax.ShapeDtypeStruct((m, D_MODEL), F32),
        compiler_params=_params("parallel", "arbitrary"),
        name="ffn_out",
    )(h, w, x1, g.reshape(1, D_MODEL))


def _kmean_kernel(k_ref, o_ref):
    t = k_ref.shape[0]
    nb = t // MOBA_BLOCK
    km = jnp.mean(k_ref[...].reshape(nb, MOBA_BLOCK, KV_WIDTH), axis=1)
    o_ref[0] = jnp.concatenate([km, jnp.zeros((LANES - nb, KV_WIDTH), F32)], axis=0)


def _kmean(k, batch):
    t = k.shape[0] // batch
    return pl.pallas_call(
        _kmean_kernel,
        grid=(batch,),
        in_specs=[pl.BlockSpec((t, KV_WIDTH), lambda b: (b, 0))],
        out_specs=pl.BlockSpec((1, LANES, KV_WIDTH), lambda b: (b, 0, 0)),
        out_shape=jax.ShapeDtypeStruct((batch, LANES, KV_WIDTH), F32),
        compiler_params=_params("parallel"),
        name="kmean",
    )(k)


def _top_rows(gate, row_id, n_valid, n_pick):
    picks = []
    for t in range(n_pick):
        mx = jnp.max(gate, axis=0, keepdims=True)
        idx = jnp.min(jnp.where(gate == mx, row_id, jnp.iinfo(jnp.int32).max), axis=0, keepdims=True)
        picks.append(jnp.where(t < n_valid, idx, -1))
        gate = jnp.where(row_id == idx, -jnp.inf, gate)
    return picks


V_ROWS = HEAD_DIM + 16
LOG2E = 1.4426950408889634


def _moba_kernel(slopes_ref, q_ref, k_ref, v_ref, km_ref, o_ref,
                 kb_ref, vt_ref, bias_ref, ua_ref, ub_ref, m_ref, acc_ref):
    kvh = pl.program_id(1)
    i = pl.program_id(2)
    nq = KV_GROUP * MOBA_BLOCK
    nb = vt_ref.shape[0]
    nb_rows = -(-nb // 8) * 8

    @pl.when(i == 0)
    def _():
        kb_ref[...] = k_ref[...].astype(BF16)
        for n in range(nb):
            vt_ref[n, 0:HEAD_DIM, :] = v_ref[n * MOBA_BLOCK:(n + 1) * MOBA_BLOCK, :].T.astype(BF16)
            vt_ref[n, HEAD_DIM:V_ROWS, :] = jnp.ones((V_ROWS - HEAD_DIM, MOBA_BLOCK), BF16)

    q = q_ref[...]
    qs = jnp.concatenate([q[:, g * HEAD_DIM:(g + 1) * HEAD_DIM] for g in range(KV_GROUP)], axis=0)
    qsb = (qs * (HEAD_DIM ** -0.5 * LOG2E)).astype(BF16)

    contract_last = (((1,), (1,)), ((), ()))
    q_hi, q_lo = _split_hi_lo(qs)
    km_hi, km_lo = _split_hi_lo(km_ref[0, 0:nb_rows, :])
    gate = (lax.dot_general(km_hi, q_hi, contract_last, preferred_element_type=F32)
            + lax.dot_general(km_hi, q_lo, contract_last, preferred_element_type=F32)
            + lax.dot_general(km_lo, q_hi, contract_last, preferred_element_type=F32))
    blk = lax.broadcasted_iota(jnp.int32, (nb_rows, nq), 0)
    picks = _top_rows(jnp.where(blk < i, gate, -jnp.inf), blk, i, MOBA_TOPK)

    qlane = lax.broadcasted_iota(jnp.int32, (1, nq), 1)
    slope = jnp.zeros((1, nq), F32)
    for g in range(KV_GROUP):
        slope = jnp.where(qlane // MOBA_BLOCK == g, slopes_ref[kvh * KV_GROUP + g] * LOG2E, slope)
    kidx = lax.broadcasted_iota(jnp.int32, (MOBA_BLOCK, nq), 0)
    bias_ref[...] = kidx.astype(F32) * slope

    m_ref[...] = jnp.full((1, nq), NEG_BIG, F32)
    acc_ref[...] = jnp.zeros((V_ROWS, nq), F32)

    def scores(n):
        start = pl.multiple_of(n * MOBA_BLOCK, MOBA_BLOCK)
        kb = kb_ref[pl.ds(start, MOBA_BLOCK), :]
        return lax.dot_general(kb, qsb, contract_last, preferred_element_type=F32) + bias_ref[...]

    def past_update(u_ref, n):
        u = u_ref[...]
        c = ((i - n) * MOBA_BLOCK).astype(F32) * slope
        sel = (picks[0] == n) | (picks[1] == n) | (picks[2] == n)
        m_old = m_ref[...]
        m_new = jnp.where(sel, jnp.maximum(m_old, jnp.max(u, axis=0, keepdims=True) - c), m_old)
        p = jnp.exp2(u - (jnp.where(sel, m_new, -NEG_BIG) + c))
        alpha = jnp.exp2(m_old - m_new)
        acc_ref[...] = alpha * acc_ref[...] + jnp.dot(vt_ref[n], p.astype(BF16), preferred_element_type=F32)
        m_ref[...] = m_new

    def own_update(u_ref):
        u = jnp.where(kidx <= qlane % MOBA_BLOCK, u_ref[...], NEG_BIG)
        m_old = m_ref[...]
        m_new = jnp.maximum(m_old, jnp.max(u, axis=0, keepdims=True))
        p = jnp.exp2(u - m_new)
        alpha = jnp.exp2(m_old - m_new)
        acc = alpha * acc_ref[...] + jnp.dot(vt_ref[i], p.astype(BF16), preferred_element_type=F32)
        out = acc[0:HEAD_DIM, :] / acc[HEAD_DIM:HEAD_DIM + 1, :]
        for g in range(KV_GROUP):
            o_ref[:, g * HEAD_DIM:(g + 1) * HEAD_DIM] = (
                out[:, g * MOBA_BLOCK:(g + 1) * MOBA_BLOCK].T.astype(o_ref.dtype))

    ua_ref[...] = scores(0)

    def two_blocks(k, carry):
        n = 2 * k
        ub_ref[...] = scores(n + 1)
        past_update(ua_ref, n)
        ua_ref[...] = scores(n + 2)
        past_update(ub_ref, n + 1)
        return carry

    lax.fori_loop(0, i // 2, two_blocks, 0)

    @pl.when(i % 2 == 1)
    def _():
        ub_ref[...] = scores(i)
        past_update(ua_ref, i - 1)
        own_update(ub_ref)

    @pl.when(i % 2 == 0)
    def _():
        own_update(ua_ref)


def _moba_prompt(q, k, v, kmean, slopes, batch):
    t = q.shape[0] // batch
    nb = t // MOBA_BLOCK
    assert nb <= LANES
    nq = KV_GROUP * MOBA_BLOCK
    grid_spec = pltpu.PrefetchScalarGridSpec(
        num_scalar_prefetch=1,
        grid=(batch, N_KV_HEADS, nb),
        in_specs=[
            pl.BlockSpec((MOBA_BLOCK, KV_GROUP * HEAD_DIM), lambda b, h, i, s: (b * nb + i, h)),
            pl.BlockSpec((t, HEAD_DIM), lambda b, h, i, s: (b, h)),
            pl.BlockSpec((t, HEAD_DIM), lambda b, h, i, s: (b, h)),
            pl.BlockSpec((1, LANES, HEAD_DIM), lambda b, h, i, s: (b, 0, h)),
        ],
        out_specs=pl.BlockSpec((MOBA_BLOCK, KV_GROUP * HEAD_DIM), lambda b, h, i, s: (b * nb + i, h)),
        scratch_shapes=[
            pltpu.VMEM((t, HEAD_DIM), BF16),
            pltpu.VMEM((nb, V_ROWS, MOBA_BLOCK), BF16),
            pltpu.VMEM((MOBA_BLOCK, nq), F32),
            pltpu.VMEM((MOBA_BLOCK, nq), F32),
            pltpu.VMEM((MOBA_BLOCK, nq), F32),
            pltpu.VMEM((1, nq), F32),
            pltpu.VMEM((V_ROWS, nq), F32),
        ],
    )
    return pl.pallas_call(
        _moba_kernel,
        grid_spec=grid_spec,
        out_shape=jax.ShapeDtypeStruct((batch * t, ATTN_WIDTH), BF16),
        compiler_params=_params("arbitrary", "arbitrary", "arbitrary"),
        name="moba_prompt",
    )(slopes, q, k, v, kmean)


def _ssd_prompt_kernel(xbc_ref, z_ref, dt_ref, cw_ref, cb_ref, dtb_ref, alog_ref, dsk_ref, nrm_ref, e_ref,
                       y_ref, st_ref, xpad_ref):
    c = pl.program_id(1)
    L = SSD_CHUNK
    head = 8

    @pl.when(c == 0)
    def _():
        st_ref[...] = jnp.zeros_like(st_ref)
        xpad_ref[0:head, :] = jnp.zeros((head, CONV_DIM), F32)

    xpad_ref[head:head + L, :] = xbc_ref[...]
    off = head - (CONV_WIDTH - 1)
    conv = cb_ref[...] + xpad_ref[off:off + L, :] * cw_ref[0:1, :]
    for i in range(1, CONV_WIDTH):
        conv = conv + xpad_ref[off + i:off + i + L, :] * cw_ref[i:i + 1, :]
    xpad_ref[0:head, :] = xpad_ref[L:L + head, :]
    act = _silu(conv)
    xs = act[:, :D_INNER]
    bm = act[:, D_INNER:D_INNER + SSD_GROUPS * D_STATE]
    cm = act[:, D_INNER + SSD_GROUPS * D_STATE:]

    dt = _softplus(dt_ref[...] + dtb_ref[...])
    la = dt * (-jnp.exp(alog_ref[...]))
    r_i = lax.broadcasted_iota(jnp.int32, (L, L), 0)
    c_i = lax.broadcasted_iota(jnp.int32, (L, L), 1)
    causal = r_i >= c_i
    tri = causal.astype(BF16)
    la_hi = la.astype(BF16)
    la_r = la - la_hi.astype(F32)
    la_mid = la_r.astype(BF16)
    la_lo = (la_r - la_mid.astype(F32)).astype(BF16)
    acs = (jnp.dot(tri, la_hi, preferred_element_type=F32) + jnp.dot(tri, la_mid, preferred_element_type=F32)
           + jnp.dot(tri, la_lo, preferred_element_type=F32))
    acs_t = acs.T
    dt_t = dt.T
    acs_last = acs[L - 1:L, :]
    e = e_ref[...]
    exp_acs_x = jnp.dot(jnp.exp(acs).astype(BF16), e, preferred_element_type=F32)
    wend_x = jnp.dot((dt * jnp.exp(acs_last - acs)).astype(BF16), e, preferred_element_type=F32)
    dec_last = jnp.exp(acs_t[:, L - 1:L])
    lane = lax.broadcasted_iota(jnp.int32, (L, LANES), 1)
    contract_last = (((1,), (1,)), ((), ()))
    contract_first = (((0,), (0,)), ((), ()))
    z = z_ref[...]

    for g in range(SSD_GROUPS):
        gs = slice(g * GROUP_WIDTH, (g + 1) * GROUP_WIDTH)
        bg = bm[:, g * D_STATE:(g + 1) * D_STATE].astype(BF16)
        cg = cm[:, g * D_STATE:(g + 1) * D_STATE].astype(BF16)
        cb = lax.dot_general(cg, bg, contract_last, preferred_element_type=F32)
        st = st_ref[0, gs, :]
        y_state = lax.dot_general(cg, st.astype(BF16), contract_last, preferred_element_type=F32)
        pairs = []
        for j in range(GROUP_WIDTH // LANES):
            col = g * (GROUP_WIDTH // LANES) + j
            ms = []
            for h in (2 * col, 2 * col + 1):
                seg = acs[:, h:h + 1] - acs_t[h:h + 1, :]
                ms.append(cb * jnp.exp(jnp.where(causal, seg, -jnp.inf)) * dt_t[h:h + 1, :])
            xcol = xs[:, col * LANES:(col + 1) * LANES]
            rhs = jnp.concatenate([jnp.where(lane < SSD_HEAD_DIM, xcol, 0.0),
                                   jnp.where(lane >= SSD_HEAD_DIM, xcol, 0.0)], axis=0).astype(BF16)
            lhs = jnp.concatenate(ms, axis=1).astype(BF16)
            pairs.append(jnp.dot(lhs, rhs, preferred_element_type=F32))
        xg = xs[:, gs]
        yg = jnp.concatenate(pairs, axis=1) + y_state * exp_acs_x[:, gs] + dsk_ref[:, gs] * xg
        yg = yg * _silu(z[:, gs])
        yg = yg * lax.rsqrt(jnp.mean(yg * yg, axis=-1, keepdims=True) + EPS)
        y_ref[:, gs] = (yg * nrm_ref[:, gs]).astype(y_ref.dtype)

        upd = lax.dot_general((xg * wend_x[:, gs]).astype(BF16), bg, contract_first, preferred_element_type=F32)
        drows = jnp.concatenate(
            [jnp.broadcast_to(dec_last[h:h + 1, :], (SSD_HEAD_DIM, D_STATE))
             for h in range(g * 8, (g + 1) * 8)], axis=0)
        st_ref[0, gs, :] = st * drows + upd


def _ssd_prompt(xbc, z, dt_raw, cw, cb, dtb, alog, dsk_x, nrm, e, batch):
    t = xbc.shape[0] // batch
    nc = t // SSD_CHUNK
    row = lambda b, c: (b * nc + c, 0)
    const = lambda b, c: (0, 0)
    return pl.pallas_call(
        _ssd_prompt_kernel,
        grid=(batch, nc),
        in_specs=[
            pl.BlockSpec((SSD_CHUNK, CONV_DIM), row),
            pl.BlockSpec((SSD_CHUNK, D_INNER), row),
            pl.BlockSpec((SSD_CHUNK, LANES), row),
            pl.BlockSpec((CONV_WIDTH, CONV_DIM), const),
            pl.BlockSpec((1, CONV_DIM), const),
            pl.BlockSpec((1, LANES), const),
            pl.BlockSpec((1, LANES), const),
            pl.BlockSpec((1, D_INNER), const),
            pl.BlockSpec((1, D_INNER), const),
            pl.BlockSpec((LANES, D_INNER), const),
        ],
        out_specs=[
            pl.BlockSpec((SSD_CHUNK, D_INNER), row),
            pl.BlockSpec((1, D_INNER, D_STATE), lambda b, c: (b, 0, 0)),
        ],
        out_shape=[
            jax.ShapeDtypeStruct((batch * t, D_INNER), BF16),
            jax.ShapeDtypeStruct((batch, D_INNER, D_STATE), F32),
        ],
        scratch_shapes=[pltpu.VMEM((SSD_CHUNK + 8, CONV_DIM), F32)],
        compiler_params=_params("arbitrary", "arbitrary"),
        name="ssd_prompt",
    )(xbc, z, dt_raw, cw, cb, dtb, alog, dsk_x, nrm, e)


KMEAN_PAGES_PER_STEP = 16


def _dec_kmean_kernel(pt_ref, *refs):
    k_refs, o_ref = refs[:-1], refs[-1]
    rows = k_refs[0].shape[1]
    pages_per_block = MOBA_BLOCK * N_KV_HEADS // rows
    for j in range(len(k_refs) // pages_per_block):
        acc = jnp.zeros((SUBLANES, HEAD_DIM), F32)
        for page in k_refs[j * pages_per_block:(j + 1) * pages_per_block]:
            acc = acc + jnp.sum(page[0].reshape(rows // SUBLANES, SUBLANES, HEAD_DIM), axis=0)
        km = (acc[0:N_KV_HEADS] + acc[N_KV_HEADS:SUBLANES]) * (1.0 / MOBA_BLOCK)
        for c in range(N_KV_HEADS):
            o_ref[0, c, j:j + 1, :] = km[c:c + 1, :]


def _dec_kmean(kc, page_table, nb):
    nseq, n_pages = page_table.shape
    pps = min(KMEAN_PAGES_PER_STEP, n_pages)
    rows = kc.shape[1]
    bps = pps * rows // (MOBA_BLOCK * N_KV_HEADS)
    page_spec = lambda j: pl.BlockSpec((1, rows, HEAD_DIM), lambda b, s, pt: (pt[b, s * pps + j], 0, 0))
    grid_spec = pltpu.PrefetchScalarGridSpec(
        num_scalar_prefetch=1,
        grid=(nseq, n_pages // pps),
        in_specs=[page_spec(j) for j in range(pps)],
        out_specs=pl.BlockSpec((1, N_KV_HEADS, bps, HEAD_DIM), lambda b, s, pt: (b, 0, s, 0)),
    )
    return pl.pallas_call(
        _dec_kmean_kernel,
        grid_spec=grid_spec,
        out_shape=jax.ShapeDtypeStruct((nseq, N_KV_HEADS, nb, HEAD_DIM), F32),
        compiler_params=_params("arbitrary", "arbitrary"),
        name="dec_kmean",
    )(page_table, *([kc] * pps))


def _dec_select_kernel(q_ref, km_ref, sel_ref):
    nb = km_ref.shape[2]
    hrow = lax.broadcasted_iota(jnp.int32, (N_HEADS, 1), 0)
    contract_last = (((1,), (1,)), ((), ()))
    q_hi, q_lo = _split_hi_lo(q_ref[0])
    gate = jnp.zeros((N_HEADS, nb), F32)
    for c in range(N_KV_HEADS):
        km_hi, km_lo = _split_hi_lo(km_ref[0, c])
        g = (lax.dot_general(q_hi, km_hi, contract_last, preferred_element_type=F32)
             + lax.dot_general(q_lo, km_hi, contract_last, preferred_element_type=F32)
             + lax.dot_general(q_hi, km_lo, contract_last, preferred_element_type=F32))
        gate = jnp.where(hrow // KV_GROUP == c, g, gate)
    blk = lax.broadcasted_iota(jnp.int32, (N_HEADS, nb), 1)
    lane = lax.broadcasted_iota(jnp.int32, (N_HEADS, LANES), 1)
    sel = jnp.zeros((N_HEADS, LANES), jnp.int32)
    for t in range(MOBA_TOPK):
        mx = jnp.max(gate, axis=-1, keepdims=True)
        idx = jnp.min(jnp.where(gate == mx, blk, nb), axis=-1, keepdims=True)
        sel = jnp.where(lane == t, idx, sel)
        gate = jnp.where(blk == idx, -jnp.inf, gate)
    sel_ref[0] = sel


def _dec_select(q, kmean):
    nseq, _, nb, _ = kmean.shape
    assert nb >= MOBA_TOPK
    return pl.pallas_call(
        _dec_select_kernel,
        grid=(nseq,),
        in_specs=[pl.BlockSpec((1, N_HEADS, HEAD_DIM), lambda b: (b, 0, 0)),
                  pl.BlockSpec((1, N_KV_HEADS, nb, HEAD_DIM), lambda b: (b, 0, 0, 0))],
        out_specs=pl.BlockSpec((1, N_HEADS, LANES), lambda b: (b, 0, 0)),
        out_shape=jax.ShapeDtypeStruct((nseq, N_HEADS, LANES), jnp.int32),
        compiler_params=_params("parallel"),
        name="dec_select",
    )(q.reshape(nseq, N_HEADS, HEAD_DIM), kmean)


def _dec_attn_kernel(pt_ref, sel_ref, slopes_ref, q_ref, kn_ref, vn_ref, *refs, past_len, pages_per_block):
    n_sel = MOBA_TOPK * pages_per_block
    k_refs, v_refs, o_ref = refs[:n_sel], refs[n_sel:2 * n_sel], refs[2 * n_sel]
    b = pl.program_id(0)
    h = pl.program_id(1)
    c = h // KV_GROUP
    rows = k_refs[0].shape[1]
    keys_per_page = rows // N_KV_HEADS
    scale = HEAD_DIM ** -0.5
    qh = q_ref[0, pl.ds(h, 1), :]
    slope = slopes_ref[h]
    r = lax.broadcasted_iota(jnp.int32, (keys_per_page, 1), 0)
    mine = pl.ds(c, keys_per_page, stride=N_KV_HEADS)
    scores = []
    for t in range(MOBA_TOPK):
        blk = sel_ref[b, h * MOBA_TOPK + t]
        for j in range(pages_per_block):
            kp = k_refs[t * pages_per_block + j][0, mine, :]
            s = jnp.sum(kp * qh, axis=-1, keepdims=True) * scale
            kpos = blk * MOBA_BLOCK + j * keys_per_page + r
            scores.append(s - slope * (past_len - kpos).astype(F32))
    kn = kn_ref[0, pl.ds(c, 1), :]
    vn = vn_ref[0, pl.ds(c, 1), :]
    s_self = jnp.sum(qh * kn, axis=-1, keepdims=True) * scale
    m = s_self
    for s in scores:
        m = jnp.maximum(m, jnp.max(s, axis=0, keepdims=True))
    w_self = jnp.exp(s_self - m)
    l = w_self
    o = w_self * vn
    for s, v_ref in zip(scores, v_refs):
        p = jnp.exp(s - m)
        l = l + jnp.sum(p, axis=0, keepdims=True)
        o = o + jnp.sum(p * v_ref[0, mine, :], axis=0, keepdims=True)
    o_ref[0, 0] = o / l


def _dec_attn(q, k_new, v_new, kc, vc, page_table, sel, slopes):
    nseq, n_pages = page_table.shape
    rows = kc.shape[1]
    ppb = MOBA_BLOCK * N_KV_HEADS // rows
    page_spec = lambda t, j: pl.BlockSpec(
        (1, rows, HEAD_DIM), lambda b, h, pt, sel, sl: (pt[b, ppb * sel[b, h * MOBA_TOPK + t] + j], 0, 0))
    page_specs = [page_spec(t, j) for t in range(MOBA_TOPK) for j in range(ppb)]
    kv_spec = pl.BlockSpec((1, N_KV_HEADS, HEAD_DIM), lambda b, h, pt, sel, sl: (b, 0, 0))
    grid_spec = pltpu.PrefetchScalarGridSpec(
        num_scalar_prefetch=3,
        grid=(nseq, N_HEADS),
        in_specs=[pl.BlockSpec((1, N_HEADS, HEAD_DIM), lambda b, h, pt, sel, sl: (b, 0, 0)), kv_spec, kv_spec]
        + page_specs + page_specs,
        out_specs=pl.BlockSpec((1, 1, 1, HEAD_DIM), lambda b, h, pt, sel, sl: (b, h, 0, 0)),
    )
    n_sel = len(page_specs)
    out = pl.pallas_call(
        functools.partial(_dec_attn_kernel, past_len=n_pages * rows // N_KV_HEADS, pages_per_block=ppb),
        grid_spec=grid_spec,
        out_shape=jax.ShapeDtypeStruct((nseq, N_HEADS, 1, HEAD_DIM), F32),
        compiler_params=_params("arbitrary", "arbitrary"),
        name="dec_attn",
    )(page_table, sel, slopes, q.reshape(nseq, N_HEADS, HEAD_DIM), k_new.reshape(nseq, N_KV_HEADS, HEAD_DIM),
      v_new.reshape(nseq, N_KV_HEADS, HEAD_DIM), *([kc] * n_sel), *([vc] * n_sel))
    return out.reshape(nseq, ATTN_WIDTH).astype(BF16)


def _ssd_step_prep_kernel(xbc_ref, cs_ref, dt_ref, cw_ref, cb_ref, dtb_ref, alog_ref, e_ref,
                          xs_ref, b_ref, c_ref, xd_ref, da_ref, cn_ref):
    x = xbc_ref[...]
    conv = cb_ref[...] + cs_ref[0] * cw_ref[0:1, :]
    for i in range(1, CONV_WIDTH - 1):
        conv = conv + cs_ref[i] * cw_ref[i:i + 1, :]
    conv = conv + x * cw_ref[CONV_WIDTH - 1:CONV_WIDTH, :]
    for i in range(CONV_WIDTH - 2):
        cn_ref[i] = cs_ref[i + 1]
    cn_ref[CONV_WIDTH - 2] = x
    act = _silu(conv)
    xs = act[:, :D_INNER]
    xs_ref[...] = xs
    b_ref[...] = act[:, D_INNER:D_INNER + SSD_GROUPS * D_STATE]
    c_ref[...] = act[:, D_INNER + SSD_GROUPS * D_STATE:]
    dt = _softplus(dt_ref[...] + dtb_ref[...])
    da = jnp.exp(dt * (-jnp.exp(alog_ref[...])))
    e = e_ref[...]
    dt_hi, dt_lo = _split_hi_lo(dt)
    da_hi, da_lo = _split_hi_lo(da)
    dt_x = jnp.dot(dt_hi, e, preferred_element_type=F32) + jnp.dot(dt_lo, e, preferred_element_type=F32)
    da_ref[...] = jnp.dot(da_hi, e, preferred_element_type=F32) + jnp.dot(da_lo, e, preferred_element_type=F32)
    xd_ref[...] = xs * dt_x


def _ssd_step_prep(xbc, conv_state_t, dt_raw, cw, cb, dtb, alog, e):
    n = xbc.shape[0]
    sds = lambda *shape: jax.ShapeDtypeStruct(shape, F32)
    return pl.pallas_call(
        _ssd_step_prep_kernel,
        out_shape=[sds(n, D_INNER), sds(n, SSD_GROUPS * D_STATE), sds(n, SSD_GROUPS * D_STATE),
                   sds(n, D_INNER), sds(n, D_INNER), sds(CONV_WIDTH - 1, n, CONV_DIM)],
        compiler_params=pltpu.CompilerParams(vmem_limit_bytes=VMEM_LIMIT),
        name="ssd_step_prep",
    )(xbc, conv_state_t, dt_raw, cw, cb, dtb, alog, e)


def _ssd_step_kernel(xd_ref, da_ref, xs_ref, z_ref, b_ref, c_ref, st_ref, dsk_ref, nrm_ref, y_ref, sn_ref):
    sub = 8
    row0 = lax.broadcasted_iota(jnp.int32, (sub, D_STATE), 0) == 0
    contract_last = (((1,), (1,)), ((), ()))
    contract_first = (((0,), (0,)), ((), ()))
    ones0 = jnp.where(row0, 1.0, 0.0).astype(BF16)
    for g in range(SSD_GROUPS):
        gs = slice(g * GROUP_WIDTH, (g + 1) * GROUP_WIDTH)
        ns = slice(g * D_STATE, (g + 1) * D_STATE)
        xd8 = jnp.broadcast_to(xd_ref[0, :, gs], (sub, GROUP_WIDTH))
        da8 = jnp.broadcast_to(da_ref[0, :, gs], (sub, GROUP_WIDTH))
        b8 = jnp.where(row0, jnp.broadcast_to(b_ref[0, :, ns], (sub, D_STATE)), 0.0)
        c8 = jnp.broadcast_to(c_ref[0, :, ns], (sub, D_STATE)).astype(BF16)
        xd_hi, xd_lo = _split_hi_lo(xd8)
        b_hi, b_lo = _split_hi_lo(b8)
        outer = lambda a, b: lax.dot_general(a, b, contract_first, preferred_element_type=F32)
        upd = outer(xd_hi, b_hi) + outer(xd_lo, b_hi) + outer(xd_hi, b_lo)
        da_hi = da8.astype(BF16)
        da_r = da8 - da_hi.astype(F32)
        da_mid = da_r.astype(BF16)
        da_lo = (da_r - da_mid.astype(F32)).astype(BF16)
        dec = outer(da_hi, ones0) + outer(da_mid, ones0) + outer(da_lo, ones0)
        st_new = st_ref[0, gs, :] * dec + upd
        sn_ref[0, gs, :] = st_new
        yg = lax.dot_general(c8, st_new.astype(BF16), contract_last, preferred_element_type=F32)[0:1, :]
        yg = yg + dsk_ref[:, gs] * xs_ref[0, :, gs]
        yg = yg * _silu(z_ref[0, :, gs])
        yg = yg * lax.rsqrt(jnp.mean(yg * yg, axis=-1, keepdims=True) + EPS)
        y_ref[0, :, gs] = (yg * nrm_ref[:, gs]).astype(y_ref.dtype)


def _ssd_step(xd, da, xs, z, bv, cv, state, dsk_x, nrm):
    n = xd.shape[0]
    r3 = lambda a: a.reshape(n, 1, a.shape[-1])
    wide = pl.BlockSpec((1, 1, D_INNER), lambda b: (b, 0, 0))
    narrow = pl.BlockSpec((1, 1, SSD_GROUPS * D_STATE), lambda b: (b, 0, 0))
    st_spec = pl.BlockSpec((1, D_INNER, D_STATE), lambda b: (b, 0, 0))
    const = pl.BlockSpec((1, D_INNER), lambda b: (0, 0))
    y, st_new = pl.pallas_call(
        _ssd_step_kernel,
        grid=(n,),
        in_specs=[wide, wide, wide, wide, narrow, narrow, st_spec, const, const],
        out_specs=[wide, st_spec],
        out_shape=[jax.ShapeDtypeStruct((n, 1, D_INNER), BF16), jax.ShapeDtypeStruct((n, D_INNER, D_STATE), F32)],
        compiler_params=_params("parallel"),
        name="ssd_step",
    )(r3(xd), r3(da), r3(xs), r3(z), r3(bv), r3(cv), state, dsk_x, nrm)
    return y.reshape(n, D_INNER), st_new


def _alibi_slopes():
    return jnp.exp2(-8.0 * jnp.arange(1, N_HEADS + 1, dtype=F32) / N_HEADS)


def _prep_weights(norm_mix, w_in, conv_w, conv_b, dt_bias, a_log, d_skip, ssd_norm, w_attn_out, w_ssd_out, w_o,
                  norm_ffn, w_ffn_in, w_ffn_out):
    w_dt = jnp.pad(w_in[:, W_IN_MAIN:].astype(BF16), ((0, 0), (0, LANES - SSD_HEADS)))
    pad_heads = lambda a: jnp.pad(a.astype(F32), (0, LANES - SSD_HEADS)).reshape(1, LANES)
    head_of_channel = jnp.arange(D_INNER, dtype=jnp.int32) // SSD_HEAD_DIM
    expand = (jnp.arange(LANES, dtype=jnp.int32)[:, None] == head_of_channel[None, :]).astype(BF16)
    return dict(
        norm_mix=norm_mix, w_in=w_in, w_dt=w_dt, conv_w=conv_w, conv_b=conv_b.reshape(1, CONV_DIM),
        dt_bias=pad_heads(dt_bias), a_log=pad_heads(a_log),
        d_skip_x=jnp.repeat(d_skip.astype(F32), SSD_HEAD_DIM).reshape(1, D_INNER),
        ssd_norm=ssd_norm.astype(F32).reshape(1, D_INNER), expand=expand,
        w_attn_out=w_attn_out.astype(BF16), w_ssd_out=w_ssd_out.astype(BF16), w_o=w_o.astype(BF16),
        norm_ffn=norm_ffn, w_ffn_in=w_ffn_in, w_ffn_out=w_ffn_out.astype(BF16),
    )


W_IN_SEGMENTS = (("gates", 2 * D_MODEL), ("q", ATTN_WIDTH), ("k", KV_WIDTH), ("v", KV_WIDTH), ("z", D_INNER),
                 ("xbc", CONV_DIM))
W_IN_MAIN = sum(size for _, size in W_IN_SEGMENTS)


def _project(x, w):
    hn = _rmsnorm(x, w["norm_mix"])
    out, col0 = {}, 0
    for name, size in W_IN_SEGMENTS:
        out[name] = _matmul_w32(hn, w["w_in"], col0, size, name="proj_" + name)
        col0 += size
    out["dt"] = _matmul(hn, w["w_dt"], name="proj_dt")
    return out


def _finish_layer(x, att, y_ssd, gates, w, norm_final):
    mixed = _mix(att, y_ssd, w["w_attn_out"], w["w_ssd_out"], gates)
    x1, hn = _oproj(mixed, w["w_o"], x, w["norm_ffn"])
    h = _ffn_in(hn, w["w_ffn_in"])
    return _ffn_out(h, w["w_ffn_out"], x1, norm_final)


def _prompt_layer(x, w, norm_final, slopes, batch):
    t = x.shape[0] // batch
    p = _project(x, w)
    kmean = _kmean(p["k"], batch)
    att = _moba_prompt(p["q"], p["k"], p["v"], kmean, slopes, batch)
    y_ssd, state = _ssd_prompt(p["xbc"], p["z"], p["dt"], w["conv_w"], w["conv_b"], w["dt_bias"], w["a_log"],
                               w["d_skip_x"], w["ssd_norm"], w["expand"], batch)
    y = _finish_layer(x, att, y_ssd, p["gates"], w, norm_final)
    conv_new = p["xbc"].reshape(batch, t, CONV_DIM)[:, t - (CONV_WIDTH - 1):, :]
    return y, p["k"], p["v"], conv_new, state


def _decode_layer(x, cache_k, cache_v, page_table, conv_state, ssm_state, w, norm_final, slopes):
    n = x.shape[0]
    p = _project(x, w)
    n_phys, page = cache_k.shape[0], cache_k.shape[1]
    kc = cache_k.reshape(n_phys, page * N_KV_HEADS, HEAD_DIM)
    vc = cache_v.reshape(n_phys, page * N_KV_HEADS, HEAD_DIM)
    nb = page_table.shape[1] * page // MOBA_BLOCK
    sel = _dec_select(p["q"], _dec_kmean(kc, page_table, nb))
    att = _dec_attn(p["q"], p["k"], p["v"], kc, vc, page_table,
                    sel[:, :, :MOBA_TOPK].reshape(n, N_HEADS * MOBA_TOPK), slopes)
    xs, bv, cv, xd, da, conv_new_t = _ssd_step_prep(
        p["xbc"], jnp.transpose(conv_state, (1, 0, 2)), p["dt"], w["conv_w"], w["conv_b"], w["dt_bias"],
        w["a_log"], w["expand"])
    y_ssd, state = _ssd_step(xd, da, xs, p["z"], bv, cv, ssm_state.reshape(n, D_INNER, D_STATE),
                             w["d_skip_x"], w["ssd_norm"])
    y = _finish_layer(x, att, y_ssd, p["gates"], w, norm_final)
    return y, p["k"], p["v"], jnp.transpose(conv_new_t, (1, 0, 2)), state


def kernel(x_prompt, x_sample, cache_k, cache_v, state_conv, state_ssm, page_table, norm_mix, w_in, conv_w, conv_b,
           dt_bias, a_log, d_skip, ssd_norm, w_attn_out, w_ssd_out, w_o, norm_ffn, w_ffn_in, w_ffn_out, norm_final):
    assert w_in.shape[0] == 1, "single-layer kernel"
    bp, t, _ = x_prompt.shape
    nd = x_sample.shape[0]
    assert x_sample.shape[1] == 1
    slopes = _alibi_slopes()
    w = _prep_weights(norm_mix[0], w_in[0], conv_w[0], conv_b[0], dt_bias[0], a_log[0], d_skip[0], ssd_norm[0],
                      w_attn_out[0], w_ssd_out[0], w_o[0], norm_ffn[0], w_ffn_in[0], w_ffn_out[0])

    yp, kp, vp, cp, hp = _prompt_layer(x_prompt.reshape(bp * t, D_MODEL), w, norm_final, slopes, bp)
    ys, ks, vs, cs, hs = _decode_layer(x_sample.reshape(nd, D_MODEL), cache_k[0], cache_v[0], page_table,
                                       state_conv[0], state_ssm[0], w, norm_final, slopes)
    kv = lambda a, b, s: a.reshape(1, b, s, N_KV_HEADS, HEAD_DIM)
    st = lambda a, b: a.reshape(1, b, SSD_HEADS, SSD_HEAD_DIM, D_STATE)
    return (yp.reshape(bp, t, D_MODEL), ys.reshape(nd, 1, D_MODEL),
            kv(kp, bp, t), kv(vp, bp, t), cp[None], st(hp, bp),
            kv(ks, nd, 1), kv(vs, nd, 1), cs[None], st(hs, nd))
```

```python
import functools

import jax
import jax.numpy as jnp
from jax import lax
from jax.experimental import pallas as pl
from jax.experimental.pallas import tpu as pltpu

F32 = jnp.float32
BF16 = jnp.bfloat16

D_MODEL = 2048
N_HEADS = 16
HEAD_DIM = 128
N_KV_HEADS = 4
KV_GROUP = N_HEADS // N_KV_HEADS
ATTN_WIDTH = N_HEADS * HEAD_DIM
KV_WIDTH = N_KV_HEADS * HEAD_DIM
MOBA_BLOCK = 256
MOBA_TOPK = 3
D_INNER = 4096
SSD_HEAD_DIM = 64
SSD_HEADS = D_INNER // SSD_HEAD_DIM
SSD_GROUPS = 8
GROUP_WIDTH = D_INNER // SSD_GROUPS
D_STATE = 128
CONV_WIDTH = 4
CONV_DIM = D_INNER + 2 * SSD_GROUPS * D_STATE
SSD_CHUNK = 128
D_FF = 5632
EPS = 1e-6
LANES = 128
SUBLANES = 8
NEG_BIG = -1e30

VMEM_LIMIT = 48 * 1024 * 1024


def _params(*sem):
    return pltpu.CompilerParams(dimension_semantics=sem, vmem_limit_bytes=VMEM_LIMIT)


def _split_hi_lo(x):
    hi = x.astype(BF16)
    lo = (x - hi.astype(F32)).astype(BF16)
    return hi, lo


def _silu(x):
    return x / (1.0 + jnp.exp(-x))


def _softplus(x):
    return jnp.maximum(x, 0.0) + jnp.log1p(jnp.exp(-jnp.abs(x)))


def _rmsnorm_kernel(x_ref, g_ref, o_ref):
    x = x_ref[...]
    y = x * lax.rsqrt(jnp.mean(x * x, axis=-1, keepdims=True) + EPS)
    o_ref[...] = (y * g_ref[...]).astype(o_ref.dtype)


def _rmsnorm(x, g, out_dtype=BF16, tm=512):
    m, d = x.shape
    tm = min(tm, m)
    return pl.pallas_call(
        _rmsnorm_kernel,
        grid=(pl.cdiv(m, tm),),
        in_specs=[pl.BlockSpec((tm, d), lambda i: (i, 0)), pl.BlockSpec((1, d), lambda i: (0, 0))],
        out_specs=pl.BlockSpec((tm, d), lambda i: (i, 0)),
        out_shape=jax.ShapeDtypeStruct((m, d), out_dtype),
        compiler_params=_params("parallel"),
        name="rmsnorm",
    )(x, g.reshape(1, d))


def _matmul_kernel(a_ref, w_ref, o_ref):
    o_ref[...] = jnp.dot(a_ref[...], w_ref[...], preferred_element_type=F32).astype(o_ref.dtype)


def _matmul(a, w, out_dtype=F32, tm=1024, tn=512, name="matmul"):
    m, k = a.shape
    n = w.shape[1]
    tm, tn = min(tm, m), min(tn, n)
    return pl.pallas_call(
        _matmul_kernel,
        grid=(pl.cdiv(m, tm), n // tn),
        in_specs=[pl.BlockSpec((tm, k), lambda i, j: (i, 0)), pl.BlockSpec((k, tn), lambda i, j: (0, j))],
        out_specs=pl.BlockSpec((tm, tn), lambda i, j: (i, j)),
        out_shape=jax.ShapeDtypeStruct((m, n), out_dtype),
        compiler_params=_params("parallel", "arbitrary"),
        name=name,
    )(a, w)


def _matmul_w32_kernel(a_ref, w_ref, o_ref, wb_ref):
    @pl.when(pl.program_id(1) == 0)
    def _():
        wb_ref[...] = w_ref[...].astype(BF16)

    o_ref[...] = jnp.dot(a_ref[...], wb_ref[...], preferred_element_type=F32).astype(o_ref.dtype)


def _matmul_w32(a, w, col0, n, out_dtype=F32, tm=1024, tn=1024, name="matmul"):
    m, k = a.shape
    tm, tn = min(tm, m), min(tn, n)
    assert col0 % tn == 0 and n % tn == 0
    return pl.pallas_call(
        _matmul_w32_kernel,
        grid=(n // tn, pl.cdiv(m, tm)),
        in_specs=[pl.BlockSpec((tm, k), lambda j, i: (i, 0)),
                  pl.BlockSpec((k, tn), lambda j, i: (0, col0 // tn + j))],
        out_specs=pl.BlockSpec((tm, tn), lambda j, i: (i, j)),
        out_shape=jax.ShapeDtypeStruct((m, n), out_dtype),
        scratch_shapes=[pltpu.VMEM((k, tn), BF16)],
        compiler_params=_params("arbitrary", "arbitrary"),
        name=name,
    )(a, w)


def _mix_kernel(att_ref, ssd_ref, wa_ref, ws_ref, ga_ref, gs_ref, o_ref):
    ya = jnp.dot(att_ref[...], wa_ref[...], preferred_element_type=F32)
    ys = jnp.dot(ssd_ref[...], ws_ref[...], preferred_element_type=F32)
    o_ref[...] = (jax.nn.sigmoid(ga_ref[...]) * ya + jax.nn.sigmoid(gs_ref[...]) * ys).astype(o_ref.dtype)


def _mix(att, ssd, wa, ws, gates, tm=1024, tn=256):
    m = att.shape[0]
    tm = min(tm, m)
    nj = D_MODEL // tn
    return pl.pallas_call(
        _mix_kernel,
        grid=(pl.cdiv(m, tm), nj),
        in_specs=[
            pl.BlockSpec((tm, ATTN_WIDTH), lambda i, j: (i, 0)),
            pl.BlockSpec((tm, D_INNER), lambda i, j: (i, 0)),
            pl.BlockSpec((ATTN_WIDTH, tn), lambda i, j: (0, j)),
            pl.BlockSpec((D_INNER, tn), lambda i, j: (0, j)),
            pl.BlockSpec((tm, tn), lambda i, j: (i, j)),
            pl.BlockSpec((tm, tn), lambda i, j: (i, j + nj)),
        ],
        out_specs=pl.BlockSpec((tm, tn), lambda i, j: (i, j)),
        out_shape=jax.ShapeDtypeStruct((m, D_MODEL), BF16),
        compiler_params=_params("parallel", "arbitrary"),
        name="mix",
    )(att, ssd, wa, ws, gates, gates)


def _oproj_kernel(mix_ref, wo_ref, x_ref, g_ref, x1_ref, hn_ref):
    x1 = x_ref[...] + jnp.dot(mix_ref[...], wo_ref[...], preferred_element_type=F32)
    x1_ref[...] = x1
    y = x1 * lax.rsqrt(jnp.mean(x1 * x1, axis=-1, keepdims=True) + EPS)
    hn_ref[...] = (y * g_ref[...]).astype(hn_ref.dtype)


def _oproj(mixed, wo, x, g, tm=512):
    m = x.shape[0]
    tm = min(tm, m)
    return pl.pallas_call(
        _oproj_kernel,
        grid=(pl.cdiv(m, tm),),
        in_specs=[
            pl.BlockSpec((tm, D_MODEL), lambda i: (i, 0)),
            pl.BlockSpec((D_MODEL, D_MODEL), lambda i: (0, 0)),
            pl.BlockSpec((tm, D_MODEL), lambda i: (i, 0)),
            pl.BlockSpec((1, D_MODEL), lambda i: (0, 0)),
        ],
        out_specs=[pl.BlockSpec((tm, D_MODEL), lambda i: (i, 0)), pl.BlockSpec((tm, D_MODEL), lambda i: (i, 0))],
        out_shape=[jax.ShapeDtypeStruct((m, D_MODEL), F32), jax.ShapeDtypeStruct((m, D_MODEL), BF16)],
        compiler_params=_params("parallel"),
        name="oproj",
    )(mixed, wo, x, g.reshape(1, D_MODEL))


def _ffn_in_kernel(h_ref, wg_ref, wu_ref, o_ref, wgb_ref, wub_ref):
    @pl.when(pl.program_id(1) == 0)
    def _():
        wgb_ref[...] = wg_ref[...].astype(BF16)
        wub_ref[...] = wu_ref[...].astype(BF16)

    h = h_ref[...]
    gate = jnp.dot(h, wgb_ref[...], preferred_element_type=F32)
    up = jnp.dot(h, wub_ref[...], preferred_element_type=F32)
    o_ref[...] = (_silu(gate) * up).astype(o_ref.dtype)


def _ffn_in(hn, w, tm=1024, tn=512):
    m = hn.shape[0]
    tm = min(tm, m)
    nj = D_FF // tn
    return pl.pallas_call(
        _ffn_in_kernel,
        grid=(nj, pl.cdiv(m, tm)),
        in_specs=[
            pl.BlockSpec((tm, D_MODEL), lambda j, i: (i, 0)),
            pl.BlockSpec((D_MODEL, tn), lambda j, i: (0, j)),
            pl.BlockSpec((D_MODEL, tn), lambda j, i: (0, j + nj)),
        ],
        out_specs=pl.BlockSpec((tm, tn), lambda j, i: (i, j)),
        out_shape=jax.ShapeDtypeStruct((m, D_FF), BF16),
        scratch_shapes=[pltpu.VMEM((D_MODEL, tn), BF16), pltpu.VMEM((D_MODEL, tn), BF16)],
        compiler_params=_params("arbitrary", "arbitrary"),
        name="ffn_in",
    )(hn, w, w)


def _ffn_out_kernel(h_ref, w_ref, x_ref, g_ref, o_ref):
    k = pl.program_id(1)

    @pl.when(k == 0)
    def _():
        o_ref[...] = x_ref[...]

    o_ref[...] += jnp.dot(h_ref[...], w_ref[...], preferred_element_type=F32)

    @pl.when(k == pl.num_programs(1) - 1)
    def _():
        x2 = o_ref[...]
        y = x2 * lax.rsqrt(jnp.mean(x2 * x2, axis=-1, keepdims=True) + EPS)
        o_ref[...] = y * g_ref[...]


def _ffn_out(h, w, x1, g, tm=512, tk=1408):
    m = h.shape[0]
    tm = min(tm, m)
    return pl.pallas_call(
        _ffn_out_kernel,
        grid=(pl.cdiv(m, tm), D_FF // tk),
        in_specs=[
            pl.BlockSpec((tm, tk), lambda i, k: (i, k)),
            pl.BlockSpec((tk, D_MODEL), lambda i, k: (k, 0)),
            pl.BlockSpec((tm, D_MODEL), lambda i, k: (i, 0)),
            pl.BlockSpec((1, D_MODEL), lambda i, k: (0, 0)),
        ],
        out_specs=pl.BlockSpec((tm, D_MODEL), lambda i, k: (i, 0)),
        out_shape=jax.ShapeDtypeStruct((m, D_MODEL), F32),
        compiler_params=_params("parallel", "arbitrary"),
        name="ffn_out",
    )(h, w, x1, g.reshape(1, D_MODEL))


def _kmean_kernel(k_ref, o_ref):
    t = k_ref.shape[0]
    nb = t // MOBA_BLOCK
    km = jnp.mean(k_ref[...].reshape(nb, MOBA_BLOCK, KV_WIDTH), axis=1)
    o_ref[0] = jnp.concatenate([km, jnp.zeros((LANES - nb, KV_WIDTH), F32)], axis=0)


def _kmean(k, batch):
    t = k.shape[0] // batch
    return pl.pallas_call(
        _kmean_kernel,
        grid=(batch,),
        in_specs=[pl.BlockSpec((t, KV_WIDTH), lambda b: (b, 0))],
        out_specs=pl.BlockSpec((1, LANES, KV_WIDTH), lambda b: (b, 0, 0)),
        out_shape=jax.ShapeDtypeStruct((batch, LANES, KV_WIDTH), F32),
        compiler_params=_params("parallel"),
        name="kmean",
    )(k)


def _top_rows(gate, row_id, n_valid, n_pick):
    picks = []
    for t in range(n_pick):
        mx = jnp.max(gate, axis=0, keepdims=True)
        idx = jnp.min(jnp.where(gate == mx, row_id, jnp.iinfo(jnp.int32).max), axis=0, keepdims=True)
        picks.append(jnp.where(t < n_valid, idx, -1))
        gate = jnp.where(row_id == idx, -jnp.inf, gate)
    return picks


V_ROWS = HEAD_DIM + 16
LOG2E = 1.4426950408889634


def _moba_kernel(slopes_ref, q_ref, k_ref, v_ref, km_ref, o_ref,
                 kb_ref, vt_ref, bias_ref, ua_ref, ub_ref, m_ref, acc_ref):
    kvh = pl.program_id(1)
    i = pl.program_id(2)
    nq = KV_GROUP * MOBA_BLOCK
    nb = vt_ref.shape[0]
    nb_rows = -(-nb // 8) * 8

    @pl.when(i == 0)
    def _():
        kb_ref[...] = k_ref[...].astype(BF16)
        for n in range(nb):
            vt_ref[n, 0:HEAD_DIM, :] = v_ref[n * MOBA_BLOCK:(n + 1) * MOBA_BLOCK, :].T.astype(BF16)
            vt_ref[n, HEAD_DIM:V_ROWS, :] = jnp.ones((V_ROWS - HEAD_DIM, MOBA_BLOCK), BF16)

    q = q_ref[...]
    qs = jnp.concatenate([q[:, g * HEAD_DIM:(g + 1) * HEAD_DIM] for g in range(KV_GROUP)], axis=0)
    qsb = (qs * (HEAD_DIM ** -0.5 * LOG2E)).astype(BF16)

    contract_last = (((1,), (1,)), ((), ()))
    q_hi, q_lo = _split_hi_lo(qs)
    km_hi, km_lo = _split_hi_lo(km_ref[0, 0:nb_rows, :])
    gate = (lax.dot_general(km_hi, q_hi, contract_last, preferred_element_type=F32)
            + lax.dot_general(km_hi, q_lo, contract_last, preferred_element_type=F32)
            + lax.dot_general(km_lo, q_hi, contract_last, preferred_element_type=F32))
    blk = lax.broadcasted_iota(jnp.int32, (nb_rows, nq), 0)
    picks = _top_rows(jnp.where(blk < i, gate, -jnp.inf), blk, i, MOBA_TOPK)

    qlane = lax.broadcasted_iota(jnp.int32, (1, nq), 1)
    slope = jnp.zeros((1, nq), F32)
    for g in range(KV_GROUP):
        slope = jnp.where(qlane // MOBA_BLOCK == g, slopes_ref[kvh * KV_GROUP + g] * LOG2E, slope)
    kidx = lax.broadcasted_iota(jnp.int32, (MOBA_BLOCK, nq), 0)
    bias_ref[...] = kidx.astype(F32) * slope

    m_ref[...] = jnp.full((1, nq), NEG_BIG, F32)
    acc_ref[...] = jnp.zeros((V_ROWS, nq), F32)

    def scores(n):
        start = pl.multiple_of(n * MOBA_BLOCK, MOBA_BLOCK)
        kb = kb_ref[pl.ds(start, MOBA_BLOCK), :]
        return lax.dot_general(kb, qsb, contract_last, preferred_element_type=F32) + bias_ref[...]

    def past_update(u_ref, n):
        u = u_ref[...]
        c = ((i - n) * MOBA_BLOCK).astype(F32) * slope
        sel = (picks[0] == n) | (picks[1] == n) | (picks[2] == n)
        m_old = m_ref[...]
        m_new = jnp.where(sel, jnp.maximum(m_old, jnp.max(u, axis=0, keepdims=True) - c), m_old)
        p = jnp.exp2(u - (jnp.where(sel, m_new, -NEG_BIG) + c))
        alpha = jnp.exp2(m_old - m_new)
        acc_ref[...] = alpha * acc_ref[...] + jnp.dot(vt_ref[n], p.astype(BF16), preferred_element_type=F32)
        m_ref[...] = m_new

    def own_update(u_ref):
        u = jnp.where(kidx <= qlane % MOBA_BLOCK, u_ref[...], NEG_BIG)
        m_old = m_ref[...]
        m_new = jnp.maximum(m_old, jnp.max(u, axis=0, keepdims=True))
        p = jnp.exp2(u - m_new)
        alpha = jnp.exp2(m_old - m_new)
        acc = alpha * acc_ref[...] + jnp.dot(vt_ref[i], p.astype(BF16), preferred_element_type=F32)
        out = acc[0:HEAD_DIM, :] / acc[HEAD_DIM:HEAD_DIM + 1, :]
        for g in range(KV_GROUP):
            o_ref[:, g * HEAD_DIM:(g + 1) * HEAD_DIM] = (
                out[:, g * MOBA_BLOCK:(g + 1) * MOBA_BLOCK].T.astype(o_ref.dtype))

    ua_ref[...] = scores(0)

    def two_blocks(k, carry):
        n = 2 * k
        ub_ref[...] = scores(n + 1)
        past_update(ua_ref, n)
        ua_ref[...] = scores(n + 2)
        past_update(ub_ref, n + 1)
        return carry

    lax.fori_loop(0, i // 2, two_blocks, 0)

    @pl.when(i % 2 == 1)
    def _():
        ub_ref[...] = scores(i)
        past_update(ua_ref, i - 1)
        own_update(ub_ref)

    @pl.when(i % 2 == 0)
    def _():
        own_update(ua_ref)


def _moba_prompt(q, k, v, kmean, slopes, batch):
    t = q.shape[0] // batch
    nb = t // MOBA_BLOCK
    assert nb <= LANES
    nq = KV_GROUP * MOBA_BLOCK
    grid_spec = pltpu.PrefetchScalarGridSpec(
        num_scalar_prefetch=1,
        grid=(batch, N_KV_HEADS, nb),
        in_specs=[
            pl.BlockSpec((MOBA_BLOCK, KV_GROUP * HEAD_DIM), lambda b, h, i, s: (b * nb + i, h)),
            pl.BlockSpec((t, HEAD_DIM), lambda b, h, i, s: (b, h)),
            pl.BlockSpec((t, HEAD_DIM), lambda b, h, i, s: (b, h)),
            pl.BlockSpec((1, LANES, HEAD_DIM), lambda b, h, i, s: (b, 0, h)),
        ],
        out_specs=pl.BlockSpec((MOBA_BLOCK, KV_GROUP * HEAD_DIM), lambda b, h, i, s: (b * nb + i, h)),
        scratch_shapes=[
            pltpu.VMEM((t, HEAD_DIM), BF16),
            pltpu.VMEM((nb, V_ROWS, MOBA_BLOCK), BF16),
            pltpu.VMEM((MOBA_BLOCK, nq), F32),
            pltpu.VMEM((MOBA_BLOCK, nq), F32),
            pltpu.VMEM((MOBA_BLOCK, nq), F32),
            pltpu.VMEM((1, nq), F32),
            pltpu.VMEM((V_ROWS, nq), F32),
        ],
    )
    return pl.pallas_call(
        _moba_kernel,
        grid_spec=grid_spec,
        out_shape=jax.ShapeDtypeStruct((batch * t, ATTN_WIDTH), BF16),
        compiler_params=_params("arbitrary", "arbitrary", "arbitrary"),
        name="moba_prompt",
    )(slopes, q, k, v, kmean)


def _ssd_prompt_kernel(xbc_ref, z_ref, dt_ref, cw_ref, cb_ref, dtb_ref, alog_ref, dsk_ref, nrm_ref, e_ref,
                       y_ref, st_ref, xpad_ref):
    c = pl.program_id(1)
    L = SSD_CHUNK
    head = 8

    @pl.when(c == 0)
    def _():
        st_ref[...] = jnp.zeros_like(st_ref)
        xpad_ref[0:head, :] = jnp.zeros((head, CONV_DIM), F32)

    xpad_ref[head:head + L, :] = xbc_ref[...]
    off = head - (CONV_WIDTH - 1)
    conv = cb_ref[...] + xpad_ref[off:off + L, :] * cw_ref[0:1, :]
    for i in range(1, CONV_WIDTH):
        conv = conv + xpad_ref[off + i:off + i + L, :] * cw_ref[i:i + 1, :]
    xpad_ref[0:head, :] = xpad_ref[L:L + head, :]
    act = _silu(conv)
    xs = act[:, :D_INNER]
    bm = act[:, D_INNER:D_INNER + SSD_GROUPS * D_STATE]
    cm = act[:, D_INNER + SSD_GROUPS * D_STATE:]

    dt = _softplus(dt_ref[...] + dtb_ref[...])
    la = dt * (-jnp.exp(alog_ref[...]))
    r_i = lax.broadcasted_iota(jnp.int32, (L, L), 0)
    c_i = lax.broadcasted_iota(jnp.int32, (L, L), 1)
    causal = r_i >= c_i
    tri = causal.astype(BF16)
    la_hi = la.astype(BF16)
    la_r = la - la_hi.astype(F32)
    la_mid = la_r.astype(BF16)
    la_lo = (la_r - la_mid.astype(F32)).astype(BF16)
    acs = (jnp.dot(tri, la_hi, preferred_element_type=F32) + jnp.dot(tri, la_mid, preferred_element_type=F32)
           + jnp.dot(tri, la_lo, preferred_element_type=F32))
    acs_t = acs.T
    dt_t = dt.T
    acs_last = acs[L - 1:L, :]
    e = e_ref[...]
    exp_acs_x = jnp.dot(jnp.exp(acs).astype(BF16), e, preferred_element_type=F32)
    wend_x = jnp.dot((dt * jnp.exp(acs_last - acs)).astype(BF16), e, preferred_element_type=F32)
    dec_last = jnp.exp(acs_t[:, L - 1:L])
    lane = lax.broadcasted_iota(jnp.int32, (L, LANES), 1)
    contract_last = (((1,), (1,)), ((), ()))
    contract_first = (((0,), (0,)), ((), ()))
    z = z_ref[...]

    for g in range(SSD_GROUPS):
        gs = slice(g * GROUP_WIDTH, (g + 1) * GROUP_WIDTH)
        bg = bm[:, g * D_STATE:(g + 1) * D_STATE].astype(BF16)
        cg = cm[:, g * D_STATE:(g + 1) * D_STATE].astype(BF16)
        cb = lax.dot_general(cg, bg, contract_last, preferred_element_type=F32)
        st = st_ref[0, gs, :]
        y_state = lax.dot_general(cg, st.astype(BF16), contract_last, preferred_element_type=F32)
        pairs = []
        for j in range(GROUP_WIDTH // LANES):
            col = g * (GROUP_WIDTH // LANES) + j
            ms = []
            for h in (2 * col, 2 * col + 1):
                seg = acs[:, h:h + 1] - acs_t[h:h + 1, :]
                ms.append(cb * jnp.exp(jnp.where(causal, seg, -jnp.inf)) * dt_t[h:h + 1, :])
            xcol = xs[:, col * LANES:(col + 1) * LANES]
            rhs = jnp.concatenate([jnp.where(lane < SSD_HEAD_DIM, xcol, 0.0),
                                   jnp.where(lane >= SSD_HEAD_DIM, xcol, 0.0)], axis=0).astype(BF16)
            lhs = jnp.concatenate(ms, axis=1).astype(BF16)
            pairs.append(jnp.dot(lhs, rhs, preferred_element_type=F32))
        xg = xs[:, gs]
        yg = jnp.concatenate(pairs, axis=1) + y_state * exp_acs_x[:, gs] + dsk_ref[:, gs] * xg
        yg = yg * _silu(z[:, gs])
        yg = yg * lax.rsqrt(jnp.mean(yg * yg, axis=-1, keepdims=True) + EPS)
        y_ref[:, gs] = (yg * nrm_ref[:, gs]).astype(y_ref.dtype)

        upd = lax.dot_general((xg * wend_x[:, gs]).astype(BF16), bg, contract_first, preferred_element_type=F32)
        drows = jnp.concatenate(
            [jnp.broadcast_to(dec_last[h:h + 1, :], (SSD_HEAD_DIM, D_STATE))
             for h in range(g * 8, (g + 1) * 8)], axis=0)
        st_ref[0, gs, :] = st * drows + upd


def _ssd_prompt(xbc, z, dt_raw, cw, cb, dtb, alog, dsk_x, nrm, e, batch):
    t = xbc.shape[0] // batch
    nc = t // SSD_CHUNK
    row = lambda b, c: (b * nc + c, 0)
    const = lambda b, c: (0, 0)
    return pl.pallas_call(
        _ssd_prompt_kernel,
        grid=(batch, nc),
        in_specs=[
            pl.BlockSpec((SSD_CHUNK, CONV_DIM), row),
            pl.BlockSpec((SSD_CHUNK, D_INNER), row),
            pl.BlockSpec((SSD_CHUNK, LANES), row),
            pl.BlockSpec((CONV_WIDTH, CONV_DIM), const),
            pl.BlockSpec((1, CONV_DIM), const),
            pl.BlockSpec((1, LANES), const),
            pl.BlockSpec((1, LANES), const),
            pl.BlockSpec((1, D_INNER), const),
            pl.BlockSpec((1, D_INNER), const),
            pl.BlockSpec((LANES, D_INNER), const),
        ],
        out_specs=[
            pl.BlockSpec((SSD_CHUNK, D_INNER), row),
            pl.BlockSpec((1, D_INNER, D_STATE), lambda b, c: (b, 0, 0)),
        ],
        out_shape=[
            jax.ShapeDtypeStruct((batch * t, D_INNER), BF16),
            jax.ShapeDtypeStruct((batch, D_INNER, D_STATE), F32),
        ],
        scratch_shapes=[pltpu.VMEM((SSD_CHUNK + 8, CONV_DIM), F32)],
        compiler_params=_params("arbitrary", "arbitrary"),
        name="ssd_prompt",
    )(xbc, z, dt_raw, cw, cb, dtb, alog, dsk_x, nrm, e)


KMEAN_PAGES_PER_STEP = 16


def _dec_kmean_kernel(pt_ref, *refs):
    k_refs, o_ref = refs[:-1], refs[-1]
    rows = k_refs[0].shape[1]
    pages_per_block = MOBA_BLOCK * N_KV_HEADS // rows
    for j in range(len(k_refs) // pages_per_block):
        acc = jnp.zeros((SUBLANES, HEAD_DIM), F32)
        for page in k_refs[j * pages_per_block:(j + 1) * pages_per_block]:
            acc = acc + jnp.sum(page[0].reshape(rows // SUBLANES, SUBLANES, HEAD_DIM), axis=0)
        km = (acc[0:N_KV_HEADS] + acc[N_KV_HEADS:SUBLANES]) * (1.0 / MOBA_BLOCK)
        for c in range(N_KV_HEADS):
            o_ref[0, c, j:j + 1, :] = km[c:c + 1, :]


def _dec_kmean(kc, page_table, nb):
    nseq, n_pages = page_table.shape
    pps = min(KMEAN_PAGES_PER_STEP, n_pages)
    rows = kc.shape[1]
    bps = pps * rows // (MOBA_BLOCK * N_KV_HEADS)
    page_spec = lambda j: pl.BlockSpec((1, rows, HEAD_DIM), lambda b, s, pt: (pt[b, s * pps + j], 0, 0))
    grid_spec = pltpu.PrefetchScalarGridSpec(
        num_scalar_prefetch=1,
        grid=(nseq, n_pages // pps),
        in_specs=[page_spec(j) for j in range(pps)],
        out_specs=pl.BlockSpec((1, N_KV_HEADS, bps, HEAD_DIM), lambda b, s, pt: (b, 0, s, 0)),
    )
    return pl.pallas_call(
        _dec_kmean_kernel,
        grid_spec=grid_spec,
        out_shape=jax.ShapeDtypeStruct((nseq, N_KV_HEADS, nb, HEAD_DIM), F32),
        compiler_params=_params("arbitrary", "arbitrary"),
        name="dec_kmean",
    )(page_table, *([kc] * pps))


def _dec_select_kernel(q_ref, km_ref, sel_ref):
    nb = km_ref.shape[2]
    hrow = lax.broadcasted_iota(jnp.int32, (N_HEADS, 1), 0)
    contract_last = (((1,), (1,)), ((), ()))
    q_hi, q_lo = _split_hi_lo(q_ref[0])
    gate = jnp.zeros((N_HEADS, nb), F32)
    for c in range(N_KV_HEADS):
        km_hi, km_lo = _split_hi_lo(km_ref[0, c])
        g = (lax.dot_general(q_hi, km_hi, contract_last, preferred_element_type=F32)
             + lax.dot_general(q_lo, km_hi, contract_last, preferred_element_type=F32)
             + lax.dot_general(q_hi, km_lo, contract_last, preferred_element_type=F32))
        gate = jnp.where(hrow // KV_GROUP == c, g, gate)
    blk = lax.broadcasted_iota(jnp.int32, (N_HEADS, nb), 1)
    lane = lax.broadcasted_iota(jnp.int32, (N_HEADS, LANES), 1)
    sel = jnp.zeros((N_HEADS, LANES), jnp.int32)
    for t in range(MOBA_TOPK):
        mx = jnp.max(gate, axis=-1, keepdims=True)
        idx = jnp.min(jnp.where(gate == mx, blk, nb), axis=-1, keepdims=True)
        sel = jnp.where(lane == t, idx, sel)
        gate = jnp.where(blk == idx, -jnp.inf, gate)
    sel_ref[0] = sel


def _dec_select(q, kmean):
    nseq, _, nb, _ = kmean.shape
    assert nb >= MOBA_TOPK
    return pl.pallas_call(
        _dec_select_kernel,
        grid=(nseq,),
        in_specs=[pl.BlockSpec((1, N_HEADS, HEAD_DIM), lambda b: (b, 0, 0)),
                  pl.BlockSpec((1, N_KV_HEADS, nb, HEAD_DIM), lambda b: (b, 0, 0, 0))],
        out_specs=pl.BlockSpec((1, N_HEADS, LANES), lambda b: (b, 0, 0)),
        out_shape=jax.ShapeDtypeStruct((nseq, N_HEADS, LANES), jnp.int32),
        compiler_params=_params("parallel"),
        name="dec_select",
    )(q.reshape(nseq, N_HEADS, HEAD_DIM), kmean)


def _dec_attn_kernel(pt_ref, sel_ref, slopes_ref, q_ref, kn_ref, vn_ref, kc_ref, vc_ref, o_ref,
                     kbuf, vbuf, sem, *, past_len, pages_per_block):
    n_sel = MOBA_TOPK * pages_per_block
    b = pl.program_id(0)
    h = pl.program_id(1)
    step = b * N_HEADS + h
    n_steps = pl.num_programs(0) * N_HEADS
    slot = step % 2
    keys_per_page = kbuf.shape[2]

    def page_copies(at_step, at_slot):
        bb = at_step // N_HEADS
        hh = at_step % N_HEADS
        cc = hh // KV_GROUP
        copies = []
        for t in range(MOBA_TOPK):
            blk = sel_ref[bb, hh * MOBA_TOPK + t]
            for j in range(pages_per_block):
                page = pt_ref[bb, pages_per_block * blk + j]
                i = t * pages_per_block + j
                copies.append(pltpu.make_async_copy(kc_ref.at[page, :, cc, :], kbuf.at[at_slot, i], sem.at[at_slot, i]))
                copies.append(pltpu.make_async_copy(vc_ref.at[page, :, cc, :], vbuf.at[at_slot, i],
                                                    sem.at[at_slot, n_sel + i]))
        return copies

    @pl.when(step == 0)
    def _():
        for copy in page_copies(step, slot):
            copy.start()

    @pl.when(step + 1 < n_steps)
    def _():
        for copy in page_copies(step + 1, 1 - slot):
            copy.start()

    for copy in page_copies(step, slot):
        copy.wait()

    c = h // KV_GROUP
    scale = HEAD_DIM ** -0.5
    qh = q_ref[0, pl.ds(h, 1), :]
    slope = slopes_ref[h]
    r = lax.broadcasted_iota(jnp.int32, (keys_per_page, 1), 0)
    scores = []
    for t in range(MOBA_TOPK):
        blk = sel_ref[b, h * MOBA_TOPK + t]
        for j in range(pages_per_block):
            kp = kbuf[slot, t * pages_per_block + j]
            s = jnp.sum(kp * qh, axis=-1, keepdims=True) * scale
            kpos = blk * MOBA_BLOCK + j * keys_per_page + r
            scores.append(s - slope * (past_len - kpos).astype(F32))
    kn = kn_ref[0, pl.ds(c, 1), :]
    vn = vn_ref[0, pl.ds(c, 1), :]
    s_self = jnp.sum(qh * kn, axis=-1, keepdims=True) * scale
    m = s_self
    for s in scores:
        m = jnp.maximum(m, jnp.max(s, axis=0, keepdims=True))
    w_self = jnp.exp(s_self - m)
    l = w_self
    o = w_self * vn
    for i, s in enumerate(scores):
        p = jnp.exp(s - m)
        l = l + jnp.sum(p, axis=0, keepdims=True)
        o = o + jnp.sum(p * vbuf[slot, i], axis=0, keepdims=True)
    o_ref[0, 0] = o / l


def _dec_attn(q, k_new, v_new, kc, vc, page_table, sel, slopes):
    nseq, n_pages = page_table.shape
    keys_per_page = kc.shape[1]
    ppb = MOBA_BLOCK // keys_per_page
    n_sel = MOBA_TOPK * ppb
    kv_spec = pl.BlockSpec((1, N_KV_HEADS, HEAD_DIM), lambda b, h, pt, sel, sl: (b, 0, 0))
    grid_spec = pltpu.PrefetchScalarGridSpec(
        num_scalar_prefetch=3,
        grid=(nseq, N_HEADS),
        in_specs=[pl.BlockSpec((1, N_HEADS, HEAD_DIM), lambda b, h, pt, sel, sl: (b, 0, 0)), kv_spec, kv_spec,
                  pl.BlockSpec(memory_space=pl.ANY), pl.BlockSpec(memory_space=pl.ANY)],
        out_specs=pl.BlockSpec((1, 1, 1, HEAD_DIM), lambda b, h, pt, sel, sl: (b, h, 0, 0)),
        scratch_shapes=[pltpu.VMEM((2, n_sel, keys_per_page, HEAD_DIM), F32),
                        pltpu.VMEM((2, n_sel, keys_per_page, HEAD_DIM), F32),
                        pltpu.SemaphoreType.DMA((2, 2 * n_sel))],
    )
    out = pl.pallas_call(
        functools.partial(_dec_attn_kernel, past_len=n_pages * keys_per_page, pages_per_block=ppb),
        grid_spec=grid_spec,
        out_shape=jax.ShapeDtypeStruct((nseq, N_HEADS, 1, HEAD_DIM), F32),
        compiler_params=_params("arbitrary", "arbitrary"),
        name="dec_attn",
    )(page_table, sel, slopes, q.reshape(nseq, N_HEADS, HEAD_DIM), k_new.reshape(nseq, N_KV_HEADS, HEAD_DIM),
      v_new.reshape(nseq, N_KV_HEADS, HEAD_DIM), kc, vc)
    return out.reshape(nseq, ATTN_WIDTH).astype(BF16)


def _ssd_step_prep_kernel(xbc_ref, cs_ref, dt_ref, cw_ref, cb_ref, dtb_ref, alog_ref, e_ref,
                          xs_ref, b_ref, c_ref, xd_ref, da_ref, cn_ref):
    x = xbc_ref[...]
    conv = cb_ref[...] + cs_ref[0] * cw_ref[0:1, :]
    for i in range(1, CONV_WIDTH - 1):
        conv = conv + cs_ref[i] * cw_ref[i:i + 1, :]
    conv = conv + x * cw_ref[CONV_WIDTH - 1:CONV_WIDTH, :]
    for i in range(CONV_WIDTH - 2):
        cn_ref[i] = cs_ref[i + 1]
    cn_ref[CONV_WIDTH - 2] = x
    act = _silu(conv)
    xs = act[:, :D_INNER]
    xs_ref[...] = xs
    b_ref[...] = act[:, D_INNER:D_INNER + SSD_GROUPS * D_STATE]
    c_ref[...] = act[:, D_INNER + SSD_GROUPS * D_STATE:]
    dt = _softplus(dt_ref[...] + dtb_ref[...])
    da = jnp.exp(dt * (-jnp.exp(alog_ref[...])))
    e = e_ref[...]
    dt_hi, dt_lo = _split_hi_lo(dt)
    da_hi, da_lo = _split_hi_lo(da)
    dt_x = jnp.dot(dt_hi, e, preferred_element_type=F32) + jnp.dot(dt_lo, e, preferred_element_type=F32)
    da_ref[...] = jnp.dot(da_hi, e, preferred_element_type=F32) + jnp.dot(da_lo, e, preferred_element_type=F32)
    xd_ref[...] = xs * dt_x


def _ssd_step_prep(xbc, conv_state_t, dt_raw, cw, cb, dtb, alog, e):
    n = xbc.shape[0]
    sds = lambda *shape: jax.ShapeDtypeStruct(shape, F32)
    return pl.pallas_call(
        _ssd_step_prep_kernel,
        out_shape=[sds(n, D_INNER), sds(n, SSD_GROUPS * D_STATE), sds(n, SSD_GROUPS * D_STATE),
                   sds(n, D_INNER), sds(n, D_INNER), sds(CONV_WIDTH - 1, n, CONV_DIM)],
        compiler_params=pltpu.CompilerParams(vmem_limit_bytes=VMEM_LIMIT),
        name="ssd_step_prep",
    )(xbc, conv_state_t, dt_raw, cw, cb, dtb, alog, e)


def _ssd_step_kernel(xd_ref, da_ref, xs_ref, z_ref, b_ref, c_ref, st_ref, dsk_ref, nrm_ref, y_ref, sn_ref):
    sub = 8
    row0 = lax.broadcasted_iota(jnp.int32, (sub, D_STATE), 0) == 0
    contract_last = (((1,), (1,)), ((), ()))
    contract_first = (((0,), (0,)), ((), ()))
    ones0 = jnp.where(row0, 1.0, 0.0).astype(BF16)
    for g in range(SSD_GROUPS):
        gs = slice(g * GROUP_WIDTH, (g + 1) * GROUP_WIDTH)
        ns = slice(g * D_STATE, (g + 1) * D_STATE)
        xd8 = jnp.broadcast_to(xd_ref[0, :, gs], (sub, GROUP_WIDTH))
        da8 = jnp.broadcast_to(da_ref[0, :, gs], (sub, GROUP_WIDTH))
        b8 = jnp.where(row0, jnp.broadcast_to(b_ref[0, :, ns], (sub, D_STATE)), 0.0)
        c8 = jnp.broadcast_to(c_ref[0, :, ns], (sub, D_STATE)).astype(BF16)
        xd_hi, xd_lo = _split_hi_lo(xd8)
        b_hi, b_lo = _split_hi_lo(b8)
        outer = lambda a, b: lax.dot_general(a, b, contract_first, preferred_element_type=F32)
        upd = outer(xd_hi, b_hi) + outer(xd_lo, b_hi) + outer(xd_hi, b_lo)
        da_hi = da8.astype(BF16)
        da_r = da8 - da_hi.astype(F32)
        da_mid = da_r.astype(BF16)
        da_lo = (da_r - da_mid.astype(F32)).astype(BF16)
        dec = outer(da_hi, ones0) + outer(da_mid, ones0) + outer(da_lo, ones0)
        st_new = st_ref[0, gs, :] * dec + upd
        sn_ref[0, gs, :] = st_new
        yg = lax.dot_general(c8, st_new.astype(BF16), contract_last, preferred_element_type=F32)[0:1, :]
        yg = yg + dsk_ref[:, gs] * xs_ref[0, :, gs]
        yg = yg * _silu(z_ref[0, :, gs])
        yg = yg * lax.rsqrt(jnp.mean(yg * yg, axis=-1, keepdims=True) + EPS)
        y_ref[0, :, gs] = (yg * nrm_ref[:, gs]).astype(y_ref.dtype)


def _ssd_step(xd, da, xs, z, bv, cv, state, dsk_x, nrm):
    n = xd.shape[0]
    r3 = lambda a: a.reshape(n, 1, a.shape[-1])
    wide = pl.BlockSpec((1, 1, D_INNER), lambda b: (b, 0, 0))
    narrow = pl.BlockSpec((1, 1, SSD_GROUPS * D_STATE), lambda b: (b, 0, 0))
    st_spec = pl.BlockSpec((1, D_INNER, D_STATE), lambda b: (b, 0, 0))
    const = pl.BlockSpec((1, D_INNER), lambda b: (0, 0))
    y, st_new = pl.pallas_call(
        _ssd_step_kernel,
        grid=(n,),
        in_specs=[wide, wide, wide, wide, narrow, narrow, st_spec, const, const],
        out_specs=[wide, st_spec],
        out_shape=[jax.ShapeDtypeStruct((n, 1, D_INNER), BF16), jax.ShapeDtypeStruct((n, D_INNER, D_STATE), F32)],
        compiler_params=_params("parallel"),
        name="ssd_step",
    )(r3(xd), r3(da), r3(xs), r3(z), r3(bv), r3(cv), state, dsk_x, nrm)
    return y.reshape(n, D_INNER), st_new


def _alibi_slopes():
    return jnp.exp2(-8.0 * jnp.arange(1, N_HEADS + 1, dtype=F32) / N_HEADS)


def _prep_weights(norm_mix, w_in, conv_w, conv_b, dt_bias, a_log, d_skip, ssd_norm, w_attn_out, w_ssd_out, w_o,
                  norm_ffn, w_ffn_in, w_ffn_out):
    w_dt = jnp.pad(w_in[:, W_IN_MAIN:].astype(BF16), ((0, 0), (0, LANES - SSD_HEADS)))
    pad_heads = lambda a: jnp.pad(a.astype(F32), (0, LANES - SSD_HEADS)).reshape(1, LANES)
    head_of_channel = jnp.arange(D_INNER, dtype=jnp.int32) // SSD_HEAD_DIM
    expand = (jnp.arange(LANES, dtype=jnp.int32)[:, None] == head_of_channel[None, :]).astype(BF16)
    return dict(
        norm_mix=norm_mix, w_in=w_in, w_dt=w_dt, conv_w=conv_w, conv_b=conv_b.reshape(1, CONV_DIM),
        dt_bias=pad_heads(dt_bias), a_log=pad_heads(a_log),
        d_skip_x=jnp.repeat(d_skip.astype(F32), SSD_HEAD_DIM).reshape(1, D_INNER),
        ssd_norm=ssd_norm.astype(F32).reshape(1, D_INNER), expand=expand,
        w_attn_out=w_attn_out.astype(BF16), w_ssd_out=w_ssd_out.astype(BF16), w_o=w_o.astype(BF16),
        norm_ffn=norm_ffn, w_ffn_in=w_ffn_in, w_ffn_out=w_ffn_out.astype(BF16),
    )


W_IN_SEGMENTS = (("gates", 2 * D_MODEL), ("q", ATTN_WIDTH), ("k", KV_WIDTH), ("v", KV_WIDTH), ("z", D_INNER),
                 ("xbc", CONV_DIM))
W_IN_MAIN = sum(size for _, size in W_IN_SEGMENTS)


def _project(x, w):
    hn = _rmsnorm(x, w["norm_mix"])
    out, col0 = {}, 0
    for name, size in W_IN_SEGMENTS:
        out[name] = _matmul_w32(hn, w["w_in"], col0, size, name="proj_" + name)
        col0 += size
    out["dt"] = _matmul(hn, w["w_dt"], name="proj_dt")
    return out


def _finish_layer(x, att, y_ssd, gates, w, norm_final):
    mixed = _mix(att, y_ssd, w["w_attn_out"], w["w_ssd_out"], gates)
    x1, hn = _oproj(mixed, w["w_o"], x, w["norm_ffn"])
    h = _ffn_in(hn, w["w_ffn_in"])
    return _ffn_out(h, w["w_ffn_out"], x1, norm_final)


def _prompt_layer(x, w, norm_final, slopes, batch):
    t = x.shape[0] // batch
    p = _project(x, w)
    kmean = _kmean(p["k"], batch)
    att = _moba_prompt(p["q"], p["k"], p["v"], kmean, slopes, batch)
    y_ssd, state = _ssd_prompt(p["xbc"], p["z"], p["dt"], w["conv_w"], w["conv_b"], w["dt_bias"], w["a_log"],
                               w["d_skip_x"], w["ssd_norm"], w["expand"], batch)
    y = _finish_layer(x, att, y_ssd, p["gates"], w, norm_final)
    conv_new = p["xbc"].reshape(batch, t, CONV_DIM)[:, t - (CONV_WIDTH - 1):, :]
    return y, p["k"], p["v"], conv_new, state


def _decode_layer(x, cache_k, cache_v, page_table, conv_state, ssm_state, w, norm_final, slopes):
    n = x.shape[0]
    p = _project(x, w)
    n_phys, page = cache_k.shape[0], cache_k.shape[1]
    kc = cache_k.reshape(n_phys, page * N_KV_HEADS, HEAD_DIM)
    nb = page_table.shape[1] * page // MOBA_BLOCK
    sel = _dec_select(p["q"], _dec_kmean(kc, page_table, nb))
    att = _dec_attn(p["q"], p["k"], p["v"], cache_k, cache_v, page_table,
                    sel[:, :, :MOBA_TOPK].reshape(n, N_HEADS * MOBA_TOPK), slopes)
    xs, bv, cv, xd, da, conv_new_t = _ssd_step_prep(
        p["xbc"], jnp.transpose(conv_state, (1, 0, 2)), p["dt"], w["conv_w"], w["conv_b"], w["dt_bias"],
        w["a_log"], w["expand"])
    y_ssd, state = _ssd_step(xd, da, xs, p["z"], bv, cv, ssm_state.reshape(n, D_INNER, D_STATE),
                             w["d_skip_x"], w["ssd_norm"])
    y = _finish_layer(x, att, y_ssd, p["gates"], w, norm_final)
    return y, p["k"], p["v"], jnp.transpose(conv_new_t, (1, 0, 2)), state


def kernel(x_prompt, x_sample, cache_k, cache_v, state_conv, state_ssm, page_table, norm_mix, w_in, conv_w, conv_b,
           dt_bias, a_log, d_skip, ssd_norm, w_attn_out, w_ssd_out, w_o, norm_ffn, w_ffn_in, w_ffn_out, norm_final):
    assert w_in.shape[0] == 1, "single-layer kernel"
    bp, t, _ = x_prompt.shape
    nd = x_sample.shape[0]
    assert x_sample.shape[1] == 1
    slopes = _alibi_slopes()
    w = _prep_weights(norm_mix[0], w_in[0], conv_w[0], conv_b[0], dt_bias[0], a_log[0], d_skip[0], ssd_norm[0],
                      w_attn_out[0], w_ssd_out[0], w_o[0], norm_ffn[0], w_ffn_in[0], w_ffn_out[0])

    yp, kp, vp, cp, hp = _prompt_layer(x_prompt.reshape(bp * t, D_MODEL), w, norm_final, slopes, bp)
    ys, ks, vs, cs, hs = _decode_layer(x_sample.reshape(nd, D_MODEL), cache_k[0], cache_v[0], page_table,
                                       state_conv[0], state_ssm[0], w, norm_final, slopes)
    kv = lambda a, b, s: a.reshape(1, b, s, N_KV_HEADS, HEAD_DIM)
    st = lambda a, b: a.reshape(1, b, SSD_HEADS, SSD_HEAD_DIM, D_STATE)
    return (yp.reshape(bp, t, D_MODEL), ys.reshape(nd, 1, D_MODEL),
            kv(kp, bp, t), kv(vp, bp, t), cp[None], st(hp, bp),
            kv(ks, nd, 1), kv(vs, nd, 1), cs[None], st(hs, nd))
```

```python
import functools

import jax
import jax.numpy as jnp
from jax import lax
from jax.experimental import pallas as pl
from jax.experimental.pallas import tpu as pltpu

F32 = jnp.float32
BF16 = jnp.bfloat16

D_MODEL = 2048
N_HEADS = 16
HEAD_DIM = 128
N_KV_HEADS = 4
KV_GROUP = N_HEADS // N_KV_HEADS
ATTN_WIDTH = N_HEADS * HEAD_DIM
KV_WIDTH = N_KV_HEADS * HEAD_DIM
MOBA_BLOCK = 256
MOBA_TOPK = 3
D_INNER = 4096
SSD_HEAD_DIM = 64
SSD_HEADS = D_INNER // SSD_HEAD_DIM
SSD_GROUPS = 8
GROUP_WIDTH = D_INNER // SSD_GROUPS
D_STATE = 128
CONV_WIDTH = 4
CONV_DIM = D_INNER + 2 * SSD_GROUPS * D_STATE
SSD_CHUNK = 128
D_FF = 5632
EPS = 1e-6
LANES = 128
SUBLANES = 8
NEG_BIG = -1e30

VMEM_LIMIT = 48 * 1024 * 1024


def _params(*sem):
    return pltpu.CompilerParams(dimension_semantics=sem, vmem_limit_bytes=VMEM_LIMIT)


def _split_hi_lo(x):
    hi = x.astype(BF16)
    lo = (x - hi.astype(F32)).astype(BF16)
    return hi, lo


def _silu(x):
    return x / (1.0 + jnp.exp(-x))


def _softplus(x):
    return jnp.maximum(x, 0.0) + jnp.log1p(jnp.exp(-jnp.abs(x)))


def _rmsnorm_kernel(x_ref, g_ref, o_ref):
    x = x_ref[...]
    y = x * lax.rsqrt(jnp.mean(x * x, axis=-1, keepdims=True) + EPS)
    o_ref[...] = (y * g_ref[...]).astype(o_ref.dtype)


def _rmsnorm(x, g, out_dtype=BF16, tm=512):
    m, d = x.shape
    tm = min(tm, m)
    return pl.pallas_call(
        _rmsnorm_kernel,
        grid=(pl.cdiv(m, tm),),
        in_specs=[pl.BlockSpec((tm, d), lambda i: (i, 0)), pl.BlockSpec((1, d), lambda i: (0, 0))],
        out_specs=pl.BlockSpec((tm, d), lambda i: (i, 0)),
        out_shape=jax.ShapeDtypeStruct((m, d), out_dtype),
        compiler_params=_params("parallel"),
        name="rmsnorm",
    )(x, g.reshape(1, d))


def _matmul_kernel(a_ref, w_ref, o_ref):
    o_ref[...] = jnp.dot(a_ref[...], w_ref[...], preferred_element_type=F32).astype(o_ref.dtype)


def _matmul(a, w, out_dtype=F32, tm=1024, tn=512, name="matmul"):
    m, k = a.shape
    n = w.shape[1]
    tm, tn = min(tm, m), min(tn, n)
    return pl.pallas_call(
        _matmul_kernel,
        grid=(pl.cdiv(m, tm), n // tn),
        in_specs=[pl.BlockSpec((tm, k), lambda i, j: (i, 0)), pl.BlockSpec((k, tn), lambda i, j: (0, j))],
        out_specs=pl.BlockSpec((tm, tn), lambda i, j: (i, j)),
        out_shape=jax.ShapeDtypeStruct((m, n), out_dtype),
        compiler_params=_params("parallel", "arbitrary"),
        name=name,
    )(a, w)


def _matmul_wt32_kernel(a_ref, a2_ref, wt_ref, o_ref, o2_ref, wb_ref):
    @pl.when(pl.program_id(1) == 0)
    def _():
        wb_ref[...] = wt_ref[...].T.astype(BF16)
        o2_ref[...] = jnp.dot(a2_ref[...], wb_ref[...], preferred_element_type=F32)

    o_ref[...] = jnp.dot(a_ref[...], wb_ref[...], preferred_element_type=F32)


def _matmul_wt32(a, a2, wt, col0, n, tm=1024, tn=1024, name="matmul"):
    m, k = a.shape
    m2 = a2.shape[0]
    tm, tn = min(tm, m), min(tn, n)
    assert col0 % tn == 0 and n % tn == 0
    return pl.pallas_call(
        _matmul_wt32_kernel,
        grid=(n // tn, pl.cdiv(m, tm)),
        in_specs=[pl.BlockSpec((tm, k), lambda j, i: (i, 0)),
                  pl.BlockSpec((m2, k), lambda j, i: (0, 0)),
                  pl.BlockSpec((tn, k), lambda j, i: (col0 // tn + j, 0))],
        out_specs=[pl.BlockSpec((tm, tn), lambda j, i: (i, j)), pl.BlockSpec((m2, tn), lambda j, i: (0, j))],
        out_shape=[jax.ShapeDtypeStruct((m, n), F32), jax.ShapeDtypeStruct((m2, n), F32)],
        scratch_shapes=[pltpu.VMEM((k, tn), BF16)],
        compiler_params=_params("arbitrary", "arbitrary"),
        name=name,
    )(a, a2, wt)


def _matmul_wt32_tail_kernel(a_ref, a2_ref, wt_ref, o_ref, o2_ref, wb_ref, *, valid):
    @pl.when(pl.program_id(0) == 0)
    def _():
        wt = jnp.concatenate([wt_ref[0:valid, :], jnp.zeros((LANES - valid, wt_ref.shape[1]), F32)], axis=0)
        wb_ref[...] = wt.T.astype(BF16)
        o2_ref[...] = jnp.dot(a2_ref[...], wb_ref[...], preferred_element_type=F32)

    o_ref[...] = jnp.dot(a_ref[...], wb_ref[...], preferred_element_type=F32)


def _matmul_wt32_tail(a, a2, wt, col0, tm=1024, name="matmul_tail"):
    m, k = a.shape
    m2 = a2.shape[0]
    valid = wt.shape[0] - col0
    assert col0 % LANES == 0 and 0 < valid < LANES and valid % SUBLANES == 0
    tm = min(tm, m)
    return pl.pallas_call(
        functools.partial(_matmul_wt32_tail_kernel, valid=valid),
        grid=(pl.cdiv(m, tm),),
        in_specs=[pl.BlockSpec((tm, k), lambda i: (i, 0)), pl.BlockSpec((m2, k), lambda i: (0, 0)),
                  pl.BlockSpec((LANES, k), lambda i: (col0 // LANES, 0))],
        out_specs=[pl.BlockSpec((tm, LANES), lambda i: (i, 0)), pl.BlockSpec((m2, LANES), lambda i: (0, 0))],
        out_shape=[jax.ShapeDtypeStruct((m, LANES), F32), jax.ShapeDtypeStruct((m2, LANES), F32)],
        scratch_shapes=[pltpu.VMEM((k, LANES), BF16)],
        compiler_params=_params("arbitrary"),
        name=name,
    )(a, a2, wt)


def _mix_kernel(att_ref, ssd_ref, wa_ref, ws_ref, ga_ref, gs_ref, o_ref):
    ya = jnp.dot(att_ref[...], wa_ref[...], preferred_element_type=F32)
    ys = jnp.dot(ssd_ref[...], ws_ref[...], preferred_element_type=F32)
    o_ref[...] = (jax.nn.sigmoid(ga_ref[...]) * ya + jax.nn.sigmoid(gs_ref[...]) * ys).astype(o_ref.dtype)


def _mix(att, ssd, wa, ws, gates, tm=1024, tn=256):
    m = att.shape[0]
    tm = min(tm, m)
    nj = D_MODEL // tn
    return pl.pallas_call(
        _mix_kernel,
        grid=(pl.cdiv(m, tm), nj),
        in_specs=[
            pl.BlockSpec((tm, ATTN_WIDTH), lambda i, j: (i, 0)),
            pl.BlockSpec((tm, D_INNER), lambda i, j: (i, 0)),
            pl.BlockSpec((ATTN_WIDTH, tn), lambda i, j: (0, j)),
            pl.BlockSpec((D_INNER, tn), lambda i, j: (0, j)),
            pl.BlockSpec((tm, tn), lambda i, j: (i, j)),
            pl.BlockSpec((tm, tn), lambda i, j: (i, j + nj)),
        ],
        out_specs=pl.BlockSpec((tm, tn), lambda i, j: (i, j)),
        out_shape=jax.ShapeDtypeStruct((m, D_MODEL), BF16),
        compiler_params=_params("parallel", "arbitrary"),
        name="mix",
    )(att, ssd, wa, ws, gates, gates)


def _oproj_kernel(mix_ref, wo_ref, x_ref, g_ref, x1_ref, hn_ref):
    x1 = x_ref[...] + jnp.dot(mix_ref[...], wo_ref[...], preferred_element_type=F32)
    x1_ref[...] = x1
    y = x1 * lax.rsqrt(jnp.mean(x1 * x1, axis=-1, keepdims=True) + EPS)
    hn_ref[...] = (y * g_ref[...]).astype(hn_ref.dtype)


def _oproj(mixed, wo, x, g, tm=512):
    m = x.shape[0]
    tm = min(tm, m)
    return pl.pallas_call(
        _oproj_kernel,
        grid=(pl.cdiv(m, tm),),
        in_specs=[
            pl.BlockSpec((tm, D_MODEL), lambda i: (i, 0)),
            pl.BlockSpec((D_MODEL, D_MODEL), lambda i: (0, 0)),
            pl.BlockSpec((tm, D_MODEL), lambda i: (i, 0)),
            pl.BlockSpec((1, D_MODEL), lambda i: (0, 0)),
        ],
        out_specs=[pl.BlockSpec((tm, D_MODEL), lambda i: (i, 0)), pl.BlockSpec((tm, D_MODEL), lambda i: (i, 0))],
        out_shape=[jax.ShapeDtypeStruct((m, D_MODEL), F32), jax.ShapeDtypeStruct((m, D_MODEL), BF16)],
        compiler_params=_params("parallel"),
        name="oproj",
    )(mixed, wo, x, g.reshape(1, D_MODEL))


def _ffn_in_kernel(h_ref, h2_ref, wg_ref, wu_ref, o_ref, o2_ref, wgb_ref, wub_ref):
    def swiglu(h):
        gate = jnp.dot(h, wgb_ref[...], preferred_element_type=F32)
        up = jnp.dot(h, wub_ref[...], preferred_element_type=F32)
        return (_silu(gate) * up).astype(o_ref.dtype)

    @pl.when(pl.program_id(1) == 0)
    def _():
        wgb_ref[...] = wg_ref[...].astype(BF16)
        wub_ref[...] = wu_ref[...].astype(BF16)
        o2_ref[...] = swiglu(h2_ref[...])

    o_ref[...] = swiglu(h_ref[...])


def _ffn_in(hn, hn2, w, tm=1024, tn=512):
    m, m2 = hn.shape[0], hn2.shape[0]
    tm = min(tm, m)
    nj = D_FF // tn
    return pl.pallas_call(
        _ffn_in_kernel,
        grid=(nj, pl.cdiv(m, tm)),
        in_specs=[
            pl.BlockSpec((tm, D_MODEL), lambda j, i: (i, 0)),
            pl.BlockSpec((m2, D_MODEL), lambda j, i: (0, 0)),
            pl.BlockSpec((D_MODEL, tn), lambda j, i: (0, j)),
            pl.BlockSpec((D_MODEL, tn), lambda j, i: (0, j + nj)),
        ],
        out_specs=[pl.BlockSpec((tm, tn), lambda j, i: (i, j)), pl.BlockSpec((m2, tn), lambda j, i: (0, j))],
        out_shape=[jax.ShapeDtypeStruct((m, D_FF), BF16), jax.ShapeDtypeStruct((m2, D_FF), BF16)],
        scratch_shapes=[pltpu.VMEM((D_MODEL, tn), BF16), pltpu.VMEM((D_MODEL, tn), BF16)],
        compiler_params=_params("arbitrary", "arbitrary"),
        name="ffn_in",
    )(hn, hn2, w, w)


def _ffn_out_kernel(h_ref, w_ref, x_ref, g_ref, o_ref):
    k = pl.program_id(1)

    @pl.when(k == 0)
    def _():
        o_ref[...] = x_ref[...]

    o_ref[...] += jnp.dot(h_ref[...], w_ref[...], preferred_element_type=F32)

    @pl.when(k == pl.num_programs(1) - 1)
    def _():
        x2 = o_ref[...]
        y = x2 * lax.rsqrt(jnp.mean(x2 * x2, axis=-1, keepdims=True) + EPS)
        o_ref[...] = y * g_ref[...]


def _ffn_out(h, w, x1, g, tm=1024, tk=512):
    m = h.shape[0]
    tm = min(tm, m)
    return pl.pallas_call(
        _ffn_out_kernel,
        grid=(pl.cdiv(m, tm), D_FF // tk),
        in_specs=[
            pl.BlockSpec((tm, tk), lambda i, k: (i, k)),
            pl.BlockSpec((tk, D_MODEL), lambda i, k: (k, 0)),
            pl.BlockSpec((tm, D_MODEL), lambda i, k: (i, 0)),
            pl.BlockSpec((1, D_MODEL), lambda i, k: (0, 0)),
        ],
        out_specs=pl.BlockSpec((tm, D_MODEL), lambda i, k: (i, 0)),
        out_shape=jax.ShapeDtypeStruct((m, D_MODEL), F32),
        compiler_params=_params("parallel", "arbitrary"),
        name="ffn_out",
    )(h, w, x1, g.reshape(1, D_MODEL))


def _kmean_kernel(k_ref, o_ref):
    t = k_ref.shape[0]
    nb = t // MOBA_BLOCK
    km = jnp.mean(k_ref[...].reshape(nb, MOBA_BLOCK, KV_WIDTH), axis=1)
    o_ref[0] = jnp.concatenate([km, jnp.zeros((LANES - nb, KV_WIDTH), F32)], axis=0)


def _kmean(k, batch):
    t = k.shape[0] // batch
    return pl.pallas_call(
        _kmean_kernel,
        grid=(batch,),
        in_specs=[pl.BlockSpec((t, KV_WIDTH), lambda b: (b, 0))],
        out_specs=pl.BlockSpec((1, LANES, KV_WIDTH), lambda b: (b, 0, 0)),
        out_shape=jax.ShapeDtypeStruct((batch, LANES, KV_WIDTH), F32),
        compiler_params=_params("parallel"),
        name="kmean",
    )(k)


def _top_rows(gate, row_id, n_valid, n_pick):
    picks = []
    for t in range(n_pick):
        mx = jnp.max(gate, axis=0, keepdims=True)
        idx = jnp.min(jnp.where(gate == mx, row_id, jnp.iinfo(jnp.int32).max), axis=0, keepdims=True)
        picks.append(jnp.where(t < n_valid, idx, -1))
        gate = jnp.where(row_id == idx, -jnp.inf, gate)
    return picks


V_ROWS = HEAD_DIM + 16
LOG2E = 1.4426950408889634


def _moba_kernel(slopes_ref, q_ref, k_ref, v_ref, km_ref, o_ref,
                 kb_ref, vt_ref, bias_ref, ua_ref, ub_ref, m_ref, acc_ref):
    kvh = pl.program_id(1)
    i = pl.program_id(2)
    nq = KV_GROUP * MOBA_BLOCK
    nb = vt_ref.shape[0]
    nb_rows = -(-nb // 8) * 8

    @pl.when(i == 0)
    def _():
        kb_ref[...] = k_ref[...].astype(BF16)
        for n in range(nb):
            vt_ref[n, 0:HEAD_DIM, :] = v_ref[n * MOBA_BLOCK:(n + 1) * MOBA_BLOCK, :].T.astype(BF16)
            vt_ref[n, HEAD_DIM:V_ROWS, :] = jnp.ones((V_ROWS - HEAD_DIM, MOBA_BLOCK), BF16)

    q = q_ref[...]
    qs = jnp.concatenate([q[:, g * HEAD_DIM:(g + 1) * HEAD_DIM] for g in range(KV_GROUP)], axis=0)
    qsb = (qs * (HEAD_DIM ** -0.5 * LOG2E)).astype(BF16)

    contract_last = (((1,), (1,)), ((), ()))
    q_hi, q_lo = _split_hi_lo(qs)
    km_hi, km_lo = _split_hi_lo(km_ref[0, 0:nb_rows, :])
    gate = (lax.dot_general(km_hi, q_hi, contract_last, preferred_element_type=F32)
            + lax.dot_general(km_hi, q_lo, contract_last, preferred_element_type=F32)
            + lax.dot_general(km_lo, q_hi, contract_last, preferred_element_type=F32))
    blk = lax.broadcasted_iota(jnp.int32, (nb_rows, nq), 0)
    picks = _top_rows(jnp.where(blk < i, gate, -jnp.inf), blk, i, MOBA_TOPK)

    qlane = lax.broadcasted_iota(jnp.int32, (1, nq), 1)
    slope = jnp.zeros((1, nq), F32)
    for g in range(KV_GROUP):
        slope = jnp.where(qlane // MOBA_BLOCK == g, slopes_ref[kvh * KV_GROUP + g] * LOG2E, slope)
    kidx = lax.broadcasted_iota(jnp.int32, (MOBA_BLOCK, nq), 0)
    bias_ref[...] = kidx.astype(F32) * slope

    m_ref[...] = jnp.full((1, nq), NEG_BIG, F32)
    acc_ref[...] = jnp.zeros((V_ROWS, nq), F32)

    def scores(n):
        start = pl.multiple_of(n * MOBA_BLOCK, MOBA_BLOCK)
        kb = kb_ref[pl.ds(start, MOBA_BLOCK), :]
        return lax.dot_general(kb, qsb, contract_last, preferred_element_type=F32) + bias_ref[...]

    def past_update(u_ref, n):
        u = u_ref[...]
        c = ((i - n) * MOBA_BLOCK).astype(F32) * slope
        sel = (picks[0] == n) | (picks[1] == n) | (picks[2] == n)
        m_old = m_ref[...]
        m_new = jnp.where(sel, jnp.maximum(m_old, jnp.max(u, axis=0, keepdims=True) - c), m_old)
        p = jnp.exp2(u - (jnp.where(sel, m_new, -NEG_BIG) + c))
        alpha = jnp.exp2(m_old - m_new)
        acc_ref[...] = alpha * acc_ref[...] + jnp.dot(vt_ref[n], p.astype(BF16), preferred_element_type=F32)
        m_ref[...] = m_new

    def own_update(u_ref):
        u = jnp.where(kidx <= qlane % MOBA_BLOCK, u_ref[...], NEG_BIG)
        m_old = m_ref[...]
        m_new = jnp.maximum(m_old, jnp.max(u, axis=0, keepdims=True))
        p = jnp.exp2(u - m_new)
        alpha = jnp.exp2(m_old - m_new)
        acc = alpha * acc_ref[...] + jnp.dot(vt_ref[i], p.astype(BF16), preferred_element_type=F32)
        out = acc[0:HEAD_DIM, :] / acc[HEAD_DIM:HEAD_DIM + 1, :]
        for g in range(KV_GROUP):
            o_ref[:, g * HEAD_DIM:(g + 1) * HEAD_DIM] = (
                out[:, g * MOBA_BLOCK:(g + 1) * MOBA_BLOCK].T.astype(o_ref.dtype))

    ua_ref[...] = scores(0)

    def two_blocks(k, carry):
        n = 2 * k
        ub_ref[...] = scores(n + 1)
        past_update(ua_ref, n)
        ua_ref[...] = scores(n + 2)
        past_update(ub_ref, n + 1)
        return carry

    lax.fori_loop(0, i // 2, two_blocks, 0)

    @pl.when(i % 2 == 1)
    def _():
        ub_ref[...] = scores(i)
        past_update(ua_ref, i - 1)
        own_update(ub_ref)

    @pl.when(i % 2 == 0)
    def _():
        own_update(ua_ref)


def _moba_prompt(q, k, v, kmean, slopes, batch):
    t = q.shape[0] // batch
    nb = t // MOBA_BLOCK
    assert nb <= LANES
    nq = KV_GROUP * MOBA_BLOCK
    grid_spec = pltpu.PrefetchScalarGridSpec(
        num_scalar_prefetch=1,
        grid=(batch, N_KV_HEADS, nb),
        in_specs=[
            pl.BlockSpec((MOBA_BLOCK, KV_GROUP * HEAD_DIM), lambda b, h, i, s: (b * nb + i, h)),
            pl.BlockSpec((t, HEAD_DIM), lambda b, h, i, s: (b, h)),
            pl.BlockSpec((t, HEAD_DIM), lambda b, h, i, s: (b, h)),
            pl.BlockSpec((1, LANES, HEAD_DIM), lambda b, h, i, s: (b, 0, h)),
        ],
        out_specs=pl.BlockSpec((MOBA_BLOCK, KV_GROUP * HEAD_DIM), lambda b, h, i, s: (b * nb + i, h)),
        scratch_shapes=[
            pltpu.VMEM((t, HEAD_DIM), BF16),
            pltpu.VMEM((nb, V_ROWS, MOBA_BLOCK), BF16),
            pltpu.VMEM((MOBA_BLOCK, nq), F32),
            pltpu.VMEM((MOBA_BLOCK, nq), F32),
            pltpu.VMEM((MOBA_BLOCK, nq), F32),
            pltpu.VMEM((1, nq), F32),
            pltpu.VMEM((V_ROWS, nq), F32),
        ],
    )
    return pl.pallas_call(
        _moba_kernel,
        grid_spec=grid_spec,
        out_shape=jax.ShapeDtypeStruct((batch * t, ATTN_WIDTH), BF16),
        compiler_params=_params("arbitrary", "arbitrary", "arbitrary"),
        name="moba_prompt",
    )(slopes, q, k, v, kmean)


def _ssd_prompt_kernel(xbc_ref, z_ref, dt_ref, cw_ref, cb_ref, dtb_ref, alog_ref, dsk_ref, nrm_ref, e_ref,
                       y_ref, st_ref, xpad_ref):
    c = pl.program_id(1)
    L = SSD_CHUNK
    head = 8

    @pl.when(c == 0)
    def _():
        st_ref[...] = jnp.zeros_like(st_ref)
        xpad_ref[0:head, :] = jnp.zeros((head, CONV_DIM), F32)

    xpad_ref[head:head + L, :] = xbc_ref[...]
    off = head - (CONV_WIDTH - 1)
    conv = cb_ref[...] + xpad_ref[off:off + L, :] * cw_ref[0:1, :]
    for i in range(1, CONV_WIDTH):
        conv = conv + xpad_ref[off + i:off + i + L, :] * cw_ref[i:i + 1, :]
    xpad_ref[0:head, :] = xpad_ref[L:L + head, :]
    act = _silu(conv)
    xs = act[:, :D_INNER]
    bm = act[:, D_INNER:D_INNER + SSD_GROUPS * D_STATE]
    cm = act[:, D_INNER + SSD_GROUPS * D_STATE:]

    dt = _softplus(dt_ref[...] + dtb_ref[...])
    la = dt * (-jnp.exp(alog_ref[...]))
    r_i = lax.broadcasted_iota(jnp.int32, (L, L), 0)
    c_i = lax.broadcasted_iota(jnp.int32, (L, L), 1)
    causal = r_i >= c_i
    tri = causal.astype(BF16)
    la_hi = la.astype(BF16)
    la_r = la - la_hi.astype(F32)
    la_mid = la_r.astype(BF16)
    la_lo = (la_r - la_mid.astype(F32)).astype(BF16)
    acs = (jnp.dot(tri, la_hi, preferred_element_type=F32) + jnp.dot(tri, la_mid, preferred_element_type=F32)
           + jnp.dot(tri, la_lo, preferred_element_type=F32))
    acs_t = acs.T
    dt_t = dt.T
    acs_last = acs[L - 1:L, :]
    e = e_ref[...]
    exp_acs_x = jnp.dot(jnp.exp(acs).astype(BF16), e, preferred_element_type=F32)
    wend_x = jnp.dot((dt * jnp.exp(acs_last - acs)).astype(BF16), e, preferred_element_type=F32)
    dec_last = jnp.exp(acs_t[:, L - 1:L])
    lane = lax.broadcasted_iota(jnp.int32, (L, LANES), 1)
    contract_last = (((1,), (1,)), ((), ()))
    contract_first = (((0,), (0,)), ((), ()))
    z = z_ref[...]

    for g in range(SSD_GROUPS):
        gs = slice(g * GROUP_WIDTH, (g + 1) * GROUP_WIDTH)
        bg = bm[:, g * D_STATE:(g + 1) * D_STATE].astype(BF16)
        cg = cm[:, g * D_STATE:(g + 1) * D_STATE].astype(BF16)
        cb = lax.dot_general(cg, bg, contract_last, preferred_element_type=F32)
        st = st_ref[0, gs, :]
        y_state = lax.dot_general(cg, st.astype(BF16), contract_last, preferred_element_type=F32)
        pairs = []
        for j in range(GROUP_WIDTH // LANES):
            col = g * (GROUP_WIDTH // LANES) + j
            ms = []
            for h in (2 * col, 2 * col + 1):
                seg = acs[:, h:h + 1] - acs_t[h:h + 1, :]
                ms.append(cb * jnp.exp(jnp.where(causal, seg, -jnp.inf)) * dt_t[h:h + 1, :])
            xcol = xs[:, col * LANES:(col + 1) * LANES]
            rhs = jnp.concatenate([jnp.where(lane < SSD_HEAD_DIM, xcol, 0.0),
                                   jnp.where(lane >= SSD_HEAD_DIM, xcol, 0.0)], axis=0).astype(BF16)
            lhs = jnp.concatenate(ms, axis=1).astype(BF16)
            pairs.append(jnp.dot(lhs, rhs, preferred_element_type=F32))
        xg = xs[:, gs]
        yg = jnp.concatenate(pairs, axis=1) + y_state * exp_acs_x[:, gs] + dsk_ref[:, gs] * xg
        yg = yg * _silu(z[:, gs])
        yg = yg * lax.rsqrt(jnp.mean(yg * yg, axis=-1, keepdims=True) + EPS)
        y_ref[:, gs] = (yg * nrm_ref[:, gs]).astype(y_ref.dtype)

        upd = lax.dot_general((xg * wend_x[:, gs]).astype(BF16), bg, contract_first, preferred_element_type=F32)
        drows = jnp.concatenate(
            [jnp.broadcast_to(dec_last[h:h + 1, :], (SSD_HEAD_DIM, D_STATE))
             for h in range(g * 8, (g + 1) * 8)], axis=0)
        st_ref[0, gs, :] = st * drows + upd


def _ssd_prompt(xbc, z, dt_raw, cw, cb, dtb, alog, dsk_x, nrm, e, batch):
    t = xbc.shape[0] // batch
    nc = t // SSD_CHUNK
    row = lambda b, c: (b * nc + c, 0)
    const = lambda b, c: (0, 0)
    return pl.pallas_call(
        _ssd_prompt_kernel,
        grid=(batch, nc),
        in_specs=[
            pl.BlockSpec((SSD_CHUNK, CONV_DIM), row),
            pl.BlockSpec((SSD_CHUNK, D_INNER), row),
            pl.BlockSpec((SSD_CHUNK, LANES), row),
            pl.BlockSpec((CONV_WIDTH, CONV_DIM), const),
            pl.BlockSpec((1, CONV_DIM), const),
            pl.BlockSpec((1, LANES), const),
            pl.BlockSpec((1, LANES), const),
            pl.BlockSpec((1, D_INNER), const),
            pl.BlockSpec((1, D_INNER), const),
            pl.BlockSpec((LANES, D_INNER), const),
        ],
        out_specs=[
            pl.BlockSpec((SSD_CHUNK, D_INNER), row),
            pl.BlockSpec((1, D_INNER, D_STATE), lambda b, c: (b, 0, 0)),
        ],
        out_shape=[
            jax.ShapeDtypeStruct((batch * t, D_INNER), BF16),
            jax.ShapeDtypeStruct((batch, D_INNER, D_STATE), F32),
        ],
        scratch_shapes=[pltpu.VMEM((SSD_CHUNK + 8, CONV_DIM), F32)],
        compiler_params=_params("arbitrary", "arbitrary"),
        name="ssd_prompt",
    )(xbc, z, dt_raw, cw, cb, dtb, alog, dsk_x, nrm, e)


KMEAN_PAGES_PER_STEP = 16


def _dec_kmean_kernel(pt_ref, *refs):
    k_refs, o_ref = refs[:-1], refs[-1]
    rows = k_refs[0].shape[1]
    pages_per_block = MOBA_BLOCK * N_KV_HEADS // rows
    for j in range(len(k_refs) // pages_per_block):
        acc = jnp.zeros((SUBLANES, HEAD_DIM), F32)
        for page in k_refs[j * pages_per_block:(j + 1) * pages_per_block]:
            acc = acc + jnp.sum(page[0].reshape(rows // SUBLANES, SUBLANES, HEAD_DIM), axis=0)
        km = (acc[0:N_KV_HEADS] + acc[N_KV_HEADS:SUBLANES]) * (1.0 / MOBA_BLOCK)
        for c in range(N_KV_HEADS):
            o_ref[0, c, j:j + 1, :] = km[c:c + 1, :]


def _dec_kmean(kc, page_table, nb):
    nseq, n_pages = page_table.shape
    pps = min(KMEAN_PAGES_PER_STEP, n_pages)
    rows = kc.shape[1]
    bps = pps * rows // (MOBA_BLOCK * N_KV_HEADS)
    page_spec = lambda j: pl.BlockSpec((1, rows, HEAD_DIM), lambda b, s, pt: (pt[b, s * pps + j], 0, 0))
    grid_spec = pltpu.PrefetchScalarGridSpec(
        num_scalar_prefetch=1,
        grid=(nseq, n_pages // pps),
        in_specs=[page_spec(j) for j in range(pps)],
        out_specs=pl.BlockSpec((1, N_KV_HEADS, bps, HEAD_DIM), lambda b, s, pt: (b, 0, s, 0)),
    )
    return pl.pallas_call(
        _dec_kmean_kernel,
        grid_spec=grid_spec,
        out_shape=jax.ShapeDtypeStruct((nseq, N_KV_HEADS, nb, HEAD_DIM), F32),
        compiler_params=_params("arbitrary", "arbitrary"),
        name="dec_kmean",
    )(page_table, *([kc] * pps))


def _dec_select_kernel(q_ref, km_ref, sel_ref):
    nb = km_ref.shape[2]
    hrow = lax.broadcasted_iota(jnp.int32, (N_HEADS, 1), 0)
    contract_last = (((1,), (1,)), ((), ()))
    q_hi, q_lo = _split_hi_lo(q_ref[0])
    gate = jnp.zeros((N_HEADS, nb), F32)
    for c in range(N_KV_HEADS):
        km_hi, km_lo = _split_hi_lo(km_ref[0, c])
        g = (lax.dot_general(q_hi, km_hi, contract_last, preferred_element_type=F32)
             + lax.dot_general(q_lo, km_hi, contract_last, preferred_element_type=F32)
             + lax.dot_general(q_hi, km_lo, contract_last, preferred_element_type=F32))
        gate = jnp.where(hrow // KV_GROUP == c, g, gate)
    blk = lax.broadcasted_iota(jnp.int32, (N_HEADS, nb), 1)
    lane = lax.broadcasted_iota(jnp.int32, (N_HEADS, LANES), 1)
    sel = jnp.zeros((N_HEADS, LANES), jnp.int32)
    for t in range(MOBA_TOPK):
        mx = jnp.max(gate, axis=-1, keepdims=True)
        idx = jnp.min(jnp.where(gate == mx, blk, nb), axis=-1, keepdims=True)
        sel = jnp.where(lane == t, idx, sel)
        gate = jnp.where(blk == idx, -jnp.inf, gate)
    sel_ref[0] = sel


def _dec_select(q, kmean):
    nseq, _, nb, _ = kmean.shape
    assert nb >= MOBA_TOPK
    return pl.pallas_call(
        _dec_select_kernel,
        grid=(nseq,),
        in_specs=[pl.BlockSpec((1, N_HEADS, HEAD_DIM), lambda b: (b, 0, 0)),
                  pl.BlockSpec((1, N_KV_HEADS, nb, HEAD_DIM), lambda b: (b, 0, 0, 0))],
        out_specs=pl.BlockSpec((1, N_HEADS, LANES), lambda b: (b, 0, 0)),
        out_shape=jax.ShapeDtypeStruct((nseq, N_HEADS, LANES), jnp.int32),
        compiler_params=_params("parallel"),
        name="dec_select",
    )(q.reshape(nseq, N_HEADS, HEAD_DIM), kmean)


def _dec_attn_kernel(pt_ref, sel_ref, slopes_ref, q_ref, kn_ref, vn_ref, kc_ref, vc_ref, o_ref,
                     kbuf, vbuf, sem, *, past_len, pages_per_block):
    n_sel = MOBA_TOPK * pages_per_block
    b = pl.program_id(0)
    h = pl.program_id(1)
    step = b * N_HEADS + h
    n_steps = pl.num_programs(0) * N_HEADS
    slot = step % 2
    keys_per_page = kbuf.shape[2]

    def page_copies(at_step, at_slot):
        bb = at_step // N_HEADS
        hh = at_step % N_HEADS
        cc = hh // KV_GROUP
        copies = []
        for t in range(MOBA_TOPK):
            blk = sel_ref[bb, hh * MOBA_TOPK + t]
            for j in range(pages_per_block):
                page = pt_ref[bb, pages_per_block * blk + j]
                i = t * pages_per_block + j
                copies.append(pltpu.make_async_copy(kc_ref.at[page, :, cc, :], kbuf.at[at_slot, i], sem.at[at_slot, i]))
                copies.append(pltpu.make_async_copy(vc_ref.at[page, :, cc, :], vbuf.at[at_slot, i],
                                                    sem.at[at_slot, n_sel + i]))
        return copies

    @pl.when(step == 0)
    def _():
        for copy in page_copies(step, slot):
            copy.start()

    @pl.when(step + 1 < n_steps)
    def _():
        for copy in page_copies(step + 1, 1 - slot):
            copy.start()

    for copy in page_copies(step, slot):
        copy.wait()

    c = h // KV_GROUP
    scale = HEAD_DIM ** -0.5
    qh = q_ref[0, pl.ds(h, 1), :]
    slope = slopes_ref[h]
    r = lax.broadcasted_iota(jnp.int32, (keys_per_page, 1), 0)
    scores = []
    for t in range(MOBA_TOPK):
        blk = sel_ref[b, h * MOBA_TOPK + t]
        for j in range(pages_per_block):
            kp = kbuf[slot, t * pages_per_block + j]
            s = jnp.sum(kp * qh, axis=-1, keepdims=True) * scale
            kpos = blk * MOBA_BLOCK + j * keys_per_page + r
            scores.append(s - slope * (past_len - kpos).astype(F32))
    kn = kn_ref[0, pl.ds(c, 1), :]
    vn = vn_ref[0, pl.ds(c, 1), :]
    s_self = jnp.sum(qh * kn, axis=-1, keepdims=True) * scale
    m = s_self
    for s in scores:
        m = jnp.maximum(m, jnp.max(s, axis=0, keepdims=True))
    w_self = jnp.exp(s_self - m)
    l = w_self
    o = w_self * vn
    for i, s in enumerate(scores):
        p = jnp.exp(s - m)
        l = l + jnp.sum(p, axis=0, keepdims=True)
        o = o + jnp.sum(p * vbuf[slot, i], axis=0, keepdims=True)
    o_ref[0, 0] = o / l


def _dec_attn(q, k_new, v_new, kc, vc, page_table, sel, slopes):
    nseq, n_pages = page_table.shape
    keys_per_page = kc.shape[1]
    ppb = MOBA_BLOCK // keys_per_page
    n_sel = MOBA_TOPK * ppb
    kv_spec = pl.BlockSpec((1, N_KV_HEADS, HEAD_DIM), lambda b, h, pt, sel, sl: (b, 0, 0))
    grid_spec = pltpu.PrefetchScalarGridSpec(
        num_scalar_prefetch=3,
        grid=(nseq, N_HEADS),
        in_specs=[pl.BlockSpec((1, N_HEADS, HEAD_DIM), lambda b, h, pt, sel, sl: (b, 0, 0)), kv_spec, kv_spec,
                  pl.BlockSpec(memory_space=pl.ANY), pl.BlockSpec(memory_space=pl.ANY)],
        out_specs=pl.BlockSpec((1, 1, 1, HEAD_DIM), lambda b, h, pt, sel, sl: (b, h, 0, 0)),
        scratch_shapes=[pltpu.VMEM((2, n_sel, keys_per_page, HEAD_DIM), F32),
                        pltpu.VMEM((2, n_sel, keys_per_page, HEAD_DIM), F32),
                        pltpu.SemaphoreType.DMA((2, 2 * n_sel))],
    )
    out = pl.pallas_call(
        functools.partial(_dec_attn_kernel, past_len=n_pages * keys_per_page, pages_per_block=ppb),
        grid_spec=grid_spec,
        out_shape=jax.ShapeDtypeStruct((nseq, N_HEADS, 1, HEAD_DIM), F32),
        compiler_params=_params("arbitrary", "arbitrary"),
        name="dec_attn",
    )(page_table, sel, slopes, q.reshape(nseq, N_HEADS, HEAD_DIM), k_new.reshape(nseq, N_KV_HEADS, HEAD_DIM),
      v_new.reshape(nseq, N_KV_HEADS, HEAD_DIM), kc, vc)
    return out.reshape(nseq, ATTN_WIDTH).astype(BF16)


def _ssd_step_prep_kernel(xbc_ref, cs_ref, dt_ref, cw_ref, cb_ref, dtb_ref, alog_ref, e_ref,
                          xs_ref, b_ref, c_ref, xd_ref, da_ref, cn_ref):
    x = xbc_ref[...]
    conv = cb_ref[...] + cs_ref[0] * cw_ref[0:1, :]
    for i in range(1, CONV_WIDTH - 1):
        conv = conv + cs_ref[i] * cw_ref[i:i + 1, :]
    conv = conv + x * cw_ref[CONV_WIDTH - 1:CONV_WIDTH, :]
    for i in range(CONV_WIDTH - 2):
        cn_ref[i] = cs_ref[i + 1]
    cn_ref[CONV_WIDTH - 2] = x
    act = _silu(conv)
    xs = act[:, :D_INNER]
    xs_ref[...] = xs
    b_ref[...] = act[:, D_INNER:D_INNER + SSD_GROUPS * D_STATE]
    c_ref[...] = act[:, D_INNER + SSD_GROUPS * D_STATE:]
    dt = _softplus(dt_ref[...] + dtb_ref[...])
    da = jnp.exp(dt * (-jnp.exp(alog_ref[...])))
    e = e_ref[...]
    dt_hi, dt_lo = _split_hi_lo(dt)
    da_hi, da_lo = _split_hi_lo(da)
    dt_x = jnp.dot(dt_hi, e, preferred_element_type=F32) + jnp.dot(dt_lo, e, preferred_element_type=F32)
    da_ref[...] = jnp.dot(da_hi, e, preferred_element_type=F32) + jnp.dot(da_lo, e, preferred_element_type=F32)
    xd_ref[...] = xs * dt_x


def _ssd_step_prep(xbc, conv_state_t, dt_raw, cw, cb, dtb, alog, e):
    n = xbc.shape[0]
    sds = lambda *shape: jax.ShapeDtypeStruct(shape, F32)
    return pl.pallas_call(
        _ssd_step_prep_kernel,
        out_shape=[sds(n, D_INNER), sds(n, SSD_GROUPS * D_STATE), sds(n, SSD_GROUPS * D_STATE),
                   sds(n, D_INNER), sds(n, D_INNER), sds(CONV_WIDTH - 1, n, CONV_DIM)],
        compiler_params=pltpu.CompilerParams(vmem_limit_bytes=VMEM_LIMIT),
        name="ssd_step_prep",
    )(xbc, conv_state_t, dt_raw, cw, cb, dtb, alog, e)


def _ssd_step_kernel(xd_ref, da_ref, xs_ref, z_ref, b_ref, c_ref, st_ref, dsk_ref, nrm_ref, y_ref, sn_ref):
    sub = 8
    row0 = lax.broadcasted_iota(jnp.int32, (sub, D_STATE), 0) == 0
    contract_last = (((1,), (1,)), ((), ()))
    contract_first = (((0,), (0,)), ((), ()))
    ones0 = jnp.where(row0, 1.0, 0.0).astype(BF16)
    for g in range(SSD_GROUPS):
        gs = slice(g * GROUP_WIDTH, (g + 1) * GROUP_WIDTH)
        ns = slice(g * D_STATE, (g + 1) * D_STATE)
        xd8 = jnp.broadcast_to(xd_ref[0, :, gs], (sub, GROUP_WIDTH))
        da8 = jnp.broadcast_to(da_ref[0, :, gs], (sub, GROUP_WIDTH))
        b8 = jnp.where(row0, jnp.broadcast_to(b_ref[0, :, ns], (sub, D_STATE)), 0.0)
        c8 = jnp.broadcast_to(c_ref[0, :, ns], (sub, D_STATE)).astype(BF16)
        xd_hi, xd_lo = _split_hi_lo(xd8)
        b_hi, b_lo = _split_hi_lo(b8)
        outer = lambda a, b: lax.dot_general(a, b, contract_first, preferred_element_type=F32)
        upd = outer(xd_hi, b_hi) + outer(xd_lo, b_hi) + outer(xd_hi, b_lo)
        da_hi = da8.astype(BF16)
        da_r = da8 - da_hi.astype(F32)
        da_mid = da_r.astype(BF16)
        da_lo = (da_r - da_mid.astype(F32)).astype(BF16)
        dec = outer(da_hi, ones0) + outer(da_mid, ones0) + outer(da_lo, ones0)
        st_new = st_ref[0, gs, :] * dec + upd
        sn_ref[0, gs, :] = st_new
        yg = lax.dot_general(c8, st_new.astype(BF16), contract_last, preferred_element_type=F32)[0:1, :]
        yg = yg + dsk_ref[:, gs] * xs_ref[0, :, gs]
        yg = yg * _silu(z_ref[0, :, gs])
        yg = yg * lax.rsqrt(jnp.mean(yg * yg, axis=-1, keepdims=True) + EPS)
        y_ref[0, :, gs] = (yg * nrm_ref[:, gs]).astype(y_ref.dtype)


def _ssd_step(xd, da, xs, z, bv, cv, state, dsk_x, nrm):
    n = xd.shape[0]
    r3 = lambda a: a.reshape(n, 1, a.shape[-1])
    wide = pl.BlockSpec((1, 1, D_INNER), lambda b: (b, 0, 0))
    narrow = pl.BlockSpec((1, 1, SSD_GROUPS * D_STATE), lambda b: (b, 0, 0))
    st_spec = pl.BlockSpec((1, D_INNER, D_STATE), lambda b: (b, 0, 0))
    const = pl.BlockSpec((1, D_INNER), lambda b: (0, 0))
    y, st_new = pl.pallas_call(
        _ssd_step_kernel,
        grid=(n,),
        in_specs=[wide, wide, wide, wide, narrow, narrow, st_spec, const, const],
        out_specs=[wide, st_spec],
        out_shape=[jax.ShapeDtypeStruct((n, 1, D_INNER), BF16), jax.ShapeDtypeStruct((n, D_INNER, D_STATE), F32)],
        compiler_params=_params("parallel"),
        name="ssd_step",
    )(r3(xd), r3(da), r3(xs), r3(z), r3(bv), r3(cv), state, dsk_x, nrm)
    return y.reshape(n, D_INNER), st_new


def _alibi_slopes():
    return jnp.exp2(-8.0 * jnp.arange(1, N_HEADS + 1, dtype=F32) / N_HEADS)


def _prep_weights(norm_mix, w_in, conv_w, conv_b, dt_bias, a_log, d_skip, ssd_norm, w_attn_out, w_ssd_out, w_o,
                  norm_ffn, w_ffn_in, w_ffn_out):
    pad_heads = lambda a: jnp.pad(a.astype(F32), (0, LANES - SSD_HEADS)).reshape(1, LANES)
    head_of_channel = jnp.arange(D_INNER, dtype=jnp.int32) // SSD_HEAD_DIM
    expand = (jnp.arange(LANES, dtype=jnp.int32)[:, None] == head_of_channel[None, :]).astype(BF16)
    return dict(
        norm_mix=norm_mix, w_in_t=w_in.T, conv_w=conv_w, conv_b=conv_b.reshape(1, CONV_DIM),
        dt_bias=pad_heads(dt_bias), a_log=pad_heads(a_log),
        d_skip_x=jnp.repeat(d_skip.astype(F32), SSD_HEAD_DIM).reshape(1, D_INNER),
        ssd_norm=ssd_norm.astype(F32).reshape(1, D_INNER), expand=expand,
        w_attn_out=w_attn_out.astype(BF16), w_ssd_out=w_ssd_out.astype(BF16), w_o=w_o.astype(BF16),
        norm_ffn=norm_ffn, w_ffn_in=w_ffn_in, w_ffn_out=w_ffn_out.astype(BF16),
    )


W_IN_SEGMENTS = (("gates", 2 * D_MODEL), ("q", ATTN_WIDTH), ("k", KV_WIDTH), ("v", KV_WIDTH), ("z", D_INNER),
                 ("xbc", CONV_DIM))
W_IN_MAIN = sum(size for _, size in W_IN_SEGMENTS)


def _project(x, x2, w):
    hn, hn2 = _rmsnorm(x, w["norm_mix"]), _rmsnorm(x2, w["norm_mix"])
    out, out2, col0 = {}, {}, 0
    for name, size in W_IN_SEGMENTS:
        out[name], out2[name] = _matmul_wt32(hn, hn2, w["w_in_t"], col0, size, name="proj_" + name)
        col0 += size
    out["dt"], out2["dt"] = _matmul_wt32_tail(hn, hn2, w["w_in_t"], col0, name="proj_dt")
    return out, out2


def _mix_oproj(x, att, y_ssd, gates, w):
    mixed = _mix(att, y_ssd, w["w_attn_out"], w["w_ssd_out"], gates)
    return _oproj(mixed, w["w_o"], x, w["norm_ffn"])


def _prompt_mixers(p, w, slopes, batch):
    t = p["q"].shape[0] // batch
    kmean = _kmean(p["k"], batch)
    att = _moba_prompt(p["q"], p["k"], p["v"], kmean, slopes, batch)
    y_ssd, state = _ssd_prompt(p["xbc"], p["z"], p["dt"], w["conv_w"], w["conv_b"], w["dt_bias"], w["a_log"],
                               w["d_skip_x"], w["ssd_norm"], w["expand"], batch)
    conv_new = p["xbc"].reshape(batch, t, CONV_DIM)[:, t - (CONV_WIDTH - 1):, :]
    return att, y_ssd, conv_new, state


def _decode_mixers(p, cache_k, cache_v, page_table, conv_state, ssm_state, w, slopes):
    n = p["q"].shape[0]
    n_phys, page = cache_k.shape[0], cache_k.shape[1]
    kc = cache_k.reshape(n_phys, page * N_KV_HEADS, HEAD_DIM)
    nb = page_table.shape[1] * page // MOBA_BLOCK
    sel = _dec_select(p["q"], _dec_kmean(kc, page_table, nb))
    att = _dec_attn(p["q"], p["k"], p["v"], cache_k, cache_v, page_table,
                    sel[:, :, :MOBA_TOPK].reshape(n, N_HEADS * MOBA_TOPK), slopes)
    xs, bv, cv, xd, da, conv_new_t = _ssd_step_prep(
        p["xbc"], jnp.transpose(conv_state, (1, 0, 2)), p["dt"], w["conv_w"], w["conv_b"], w["dt_bias"],
        w["a_log"], w["expand"])
    y_ssd, state = _ssd_step(xd, da, xs, p["z"], bv, cv, ssm_state.reshape(n, D_INNER, D_STATE),
                             w["d_skip_x"], w["ssd_norm"])
    return att, y_ssd, jnp.transpose(conv_new_t, (1, 0, 2)), state


def kernel(x_prompt, x_sample, cache_k, cache_v, state_conv, state_ssm, page_table, norm_mix, w_in, conv_w, conv_b,
           dt_bias, a_log, d_skip, ssd_norm, w_attn_out, w_ssd_out, w_o, norm_ffn, w_ffn_in, w_ffn_out, norm_final):
    assert w_in.shape[0] == 1, "single-layer kernel"
    bp, t, _ = x_prompt.shape
    nd = x_sample.shape[0]
    assert x_sample.shape[1] == 1
    slopes = _alibi_slopes()
    w = _prep_weights(norm_mix[0], w_in[0], conv_w[0], conv_b[0], dt_bias[0], a_log[0], d_skip[0], ssd_norm[0],
                      w_attn_out[0], w_ssd_out[0], w_o[0], norm_ffn[0], w_ffn_in[0], w_ffn_out[0])

    xp, xs = x_prompt.reshape(bp * t, D_MODEL), x_sample.reshape(nd, D_MODEL)
    pp, ps = _project(xp, xs, w)
    att_p, ssd_p, cp, hp = _prompt_mixers(pp, w, slopes, bp)
    att_s, ssd_s, cs, hs = _decode_mixers(ps, cache_k[0], cache_v[0], page_table, state_conv[0], state_ssm[0],
                                          w, slopes)
    x1p, hnp = _mix_oproj(xp, att_p, ssd_p, pp["gates"], w)
    x1s, hns = _mix_oproj(xs, att_s, ssd_s, ps["gates"], w)
    ffp, ffs = _ffn_in(hnp, hns, w["w_ffn_in"])
    yp = _ffn_out(ffp, w["w_ffn_out"], x1p, norm_final)
    ys = _ffn_out(ffs, w["w_ffn_out"], x1s, norm_final)
    kp, vp, ks, vs = pp["k"], pp["v"], ps["k"], ps["v"]
    kv = lambda a, b, s: a.reshape(1, b, s, N_KV_HEADS, HEAD_DIM)
    st = lambda a, b: a.reshape(1, b, SSD_HEADS, SSD_HEAD_DIM, D_STATE)
    return (yp.reshape(bp, t, D_MODEL), ys.reshape(nd, 1, D_MODEL),
            kv(kp, bp, t), kv(vp, bp, t), cp[None], st(hp, bp),
            kv(ks, nd, 1), kv(vs, nd, 1), cs[None], st(hs, nd))
```

```python
import functools

import jax
import jax.numpy as jnp
from jax import lax
from jax.experimental import pallas as pl
from jax.experimental.pallas import tpu as pltpu

F32 = jnp.float32
BF16 = jnp.bfloat16

D_MODEL = 2048
N_HEADS = 16
HEAD_DIM = 128
N_KV_HEADS = 4
KV_GROUP = N_HEADS // N_KV_HEADS
ATTN_WIDTH = N_HEADS * HEAD_DIM
KV_WIDTH = N_KV_HEADS * HEAD_DIM
MOBA_BLOCK = 256
MOBA_TOPK = 3
D_INNER = 4096
SSD_HEAD_DIM = 64
SSD_HEADS = D_INNER // SSD_HEAD_DIM
SSD_GROUPS = 8
GROUP_WIDTH = D_INNER // SSD_GROUPS
D_STATE = 128
CONV_WIDTH = 4
CONV_DIM = D_INNER + 2 * SSD_GROUPS * D_STATE
SSD_CHUNK = 128
D_FF = 5632
EPS = 1e-6
LANES = 128
SUBLANES = 8
NEG_BIG = -1e30

VMEM_LIMIT = 48 * 1024 * 1024


def _params(*sem):
    return pltpu.CompilerParams(dimension_semantics=sem, vmem_limit_bytes=VMEM_LIMIT)


def _split_hi_lo(x):
    hi = x.astype(BF16)
    lo = (x - hi.astype(F32)).astype(BF16)
    return hi, lo


def _silu(x):
    return x / (1.0 + jnp.exp(-x))


def _softplus(x):
    return jnp.maximum(x, 0.0) + jnp.log1p(jnp.exp(-jnp.abs(x)))


def _rmsnorm_kernel(x_ref, g_ref, o_ref):
    x = x_ref[...]
    y = x * lax.rsqrt(jnp.mean(x * x, axis=-1, keepdims=True) + EPS)
    o_ref[...] = (y * g_ref[...]).astype(o_ref.dtype)


def _rmsnorm(x, g, out_dtype=BF16, tm=512):
    m, d = x.shape
    tm = min(tm, m)
    return pl.pallas_call(
        _rmsnorm_kernel,
        grid=(pl.cdiv(m, tm),),
        in_specs=[pl.BlockSpec((tm, d), lambda i: (i, 0)), pl.BlockSpec((1, d), lambda i: (0, 0))],
        out_specs=pl.BlockSpec((tm, d), lambda i: (i, 0)),
        out_shape=jax.ShapeDtypeStruct((m, d), out_dtype),
        compiler_params=_params("parallel"),
        name="rmsnorm",
    )(x, g.reshape(1, d))


def _matmul_kernel(a_ref, w_ref, o_ref):
    o_ref[...] = jnp.dot(a_ref[...], w_ref[...], preferred_element_type=F32).astype(o_ref.dtype)


def _matmul(a, w, out_dtype=F32, tm=1024, tn=512, name="matmul"):
    m, k = a.shape
    n = w.shape[1]
    tm, tn = min(tm, m), min(tn, n)
    return pl.pallas_call(
        _matmul_kernel,
        grid=(pl.cdiv(m, tm), n // tn),
        in_specs=[pl.BlockSpec((tm, k), lambda i, j: (i, 0)), pl.BlockSpec((k, tn), lambda i, j: (0, j))],
        out_specs=pl.BlockSpec((tm, tn), lambda i, j: (i, j)),
        out_shape=jax.ShapeDtypeStruct((m, n), out_dtype),
        compiler_params=_params("parallel", "arbitrary"),
        name=name,
    )(a, w)


def _matmul_wt32_kernel(a_ref, a2_ref, wt_ref, o_ref, o2_ref, wb_ref):
    @pl.when(pl.program_id(1) == 0)
    def _():
        wb_ref[...] = wt_ref[...].T.astype(BF16)
        o2_ref[...] = jnp.dot(a2_ref[...], wb_ref[...], preferred_element_type=F32)

    o_ref[...] = jnp.dot(a_ref[...], wb_ref[...], preferred_element_type=F32)


def _matmul_wt32(a, a2, wt, col0, n, tm=1024, tn=1024, name="matmul"):
    m, k = a.shape
    m2 = a2.shape[0]
    tm, tn = min(tm, m), min(tn, n)
    assert col0 % tn == 0 and n % tn == 0
    return pl.pallas_call(
        _matmul_wt32_kernel,
        grid=(n // tn, pl.cdiv(m, tm)),
        in_specs=[pl.BlockSpec((tm, k), lambda j, i: (i, 0)),
                  pl.BlockSpec((m2, k), lambda j, i: (0, 0)),
                  pl.BlockSpec((tn, k), lambda j, i: (col0 // tn + j, 0))],
        out_specs=[pl.BlockSpec((tm, tn), lambda j, i: (i, j)), pl.BlockSpec((m2, tn), lambda j, i: (0, j))],
        out_shape=[jax.ShapeDtypeStruct((m, n), F32), jax.ShapeDtypeStruct((m2, n), F32)],
        scratch_shapes=[pltpu.VMEM((k, tn), BF16)],
        compiler_params=_params("arbitrary", "arbitrary"),
        name=name,
    )(a, a2, wt)


def _matmul_wt32_tail_kernel(a_ref, a2_ref, wt_ref, o_ref, o2_ref, wb_ref, *, valid):
    @pl.when(pl.program_id(0) == 0)
    def _():
        wt = jnp.concatenate([wt_ref[0:valid, :], jnp.zeros((LANES - valid, wt_ref.shape[1]), F32)], axis=0)
        wb_ref[...] = wt.T.astype(BF16)
        o2_ref[...] = jnp.dot(a2_ref[...], wb_ref[...], preferred_element_type=F32)

    o_ref[...] = jnp.dot(a_ref[...], wb_ref[...], preferred_element_type=F32)


def _matmul_wt32_tail(a, a2, wt, col0, tm=1024, name="matmul_tail"):
    m, k = a.shape
    m2 = a2.shape[0]
    valid = wt.shape[0] - col0
    assert col0 % LANES == 0 and 0 < valid < LANES and valid % SUBLANES == 0
    tm = min(tm, m)
    return pl.pallas_call(
        functools.partial(_matmul_wt32_tail_kernel, valid=valid),
        grid=(pl.cdiv(m, tm),),
        in_specs=[pl.BlockSpec((tm, k), lambda i: (i, 0)), pl.BlockSpec((m2, k), lambda i: (0, 0)),
                  pl.BlockSpec((LANES, k), lambda i: (col0 // LANES, 0))],
        out_specs=[pl.BlockSpec((tm, LANES), lambda i: (i, 0)), pl.BlockSpec((m2, LANES), lambda i: (0, 0))],
        out_shape=[jax.ShapeDtypeStruct((m, LANES), F32), jax.ShapeDtypeStruct((m2, LANES), F32)],
        scratch_shapes=[pltpu.VMEM((k, LANES), BF16)],
        compiler_params=_params("arbitrary"),
        name=name,
    )(a, a2, wt)


def _mix_kernel(att_ref, ssd_ref, wa_ref, ws_ref, ga_ref, gs_ref, o_ref):
    ya = jnp.dot(att_ref[...], wa_ref[...], preferred_element_type=F32)
    ys = jnp.dot(ssd_ref[...], ws_ref[...], preferred_element_type=F32)
    o_ref[...] = (jax.nn.sigmoid(ga_ref[...]) * ya + jax.nn.sigmoid(gs_ref[...]) * ys).astype(o_ref.dtype)


def _mix(att, ssd, wa, ws, gates, tm=1024, tn=256):
    m = att.shape[0]
    tm = min(tm, m)
    nj = D_MODEL // tn
    return pl.pallas_call(
        _mix_kernel,
        grid=(pl.cdiv(m, tm), nj),
        in_specs=[
            pl.BlockSpec((tm, ATTN_WIDTH), lambda i, j: (i, 0)),
            pl.BlockSpec((tm, D_INNER), lambda i, j: (i, 0)),
            pl.BlockSpec((ATTN_WIDTH, tn), lambda i, j: (0, j)),
            pl.BlockSpec((D_INNER, tn), lambda i, j: (0, j)),
            pl.BlockSpec((tm, tn), lambda i, j: (i, j)),
            pl.BlockSpec((tm, tn), lambda i, j: (i, j + nj)),
        ],
        out_specs=pl.BlockSpec((tm, tn), lambda i, j: (i, j)),
        out_shape=jax.ShapeDtypeStruct((m, D_MODEL), BF16),
        compiler_params=_params("parallel", "arbitrary"),
        name="mix",
    )(att, ssd, wa, ws, gates, gates)


def _oproj_kernel(mix_ref, wo_ref, x_ref, g_ref, x1_ref, hn_ref):
    x1 = x_ref[...] + jnp.dot(mix_ref[...], wo_ref[...], preferred_element_type=F32)
    x1_ref[...] = x1
    y = x1 * lax.rsqrt(jnp.mean(x1 * x1, axis=-1, keepdims=True) + EPS)
    hn_ref[...] = (y * g_ref[...]).astype(hn_ref.dtype)


def _oproj(mixed, wo, x, g, tm=512):
    m = x.shape[0]
    tm = min(tm, m)
    return pl.pallas_call(
        _oproj_kernel,
        grid=(pl.cdiv(m, tm),),
        in_specs=[
            pl.BlockSpec((tm, D_MODEL), lambda i: (i, 0)),
            pl.BlockSpec((D_MODEL, D_MODEL), lambda i: (0, 0)),
            pl.BlockSpec((tm, D_MODEL), lambda i: (i, 0)),
            pl.BlockSpec((1, D_MODEL), lambda i: (0, 0)),
        ],
        out_specs=[pl.BlockSpec((tm, D_MODEL), lambda i: (i, 0)), pl.BlockSpec((tm, D_MODEL), lambda i: (i, 0))],
        out_shape=[jax.ShapeDtypeStruct((m, D_MODEL), F32), jax.ShapeDtypeStruct((m, D_MODEL), BF16)],
        compiler_params=_params("parallel"),
        name="oproj",
    )(mixed, wo, x, g.reshape(1, D_MODEL))


def _ffn_in_kernel(h_ref, h2_ref, wg_ref, wu_ref, o_ref, o2_ref, wgb_ref, wub_ref):
    def swiglu(h):
        gate = jnp.dot(h, wgb_ref[...], preferred_element_type=F32)
        up = jnp.dot(h, wub_ref[...], preferred_element_type=F32)
        return (_silu(gate) * up).astype(o_ref.dtype)

    @pl.when(pl.program_id(1) == 0)
    def _():
        wgb_ref[...] = wg_ref[...].astype(BF16)
        wub_ref[...] = wu_ref[...].astype(BF16)
        o2_ref[...] = swiglu(h2_ref[...])

    o_ref[...] = swiglu(h_ref[...])


def _ffn_in(hn, hn2, w, tm=1024, tn=512):
    m, m2 = hn.shape[0], hn2.shape[0]
    tm = min(tm, m)
    nj = D_FF // tn
    return pl.pallas_call(
        _ffn_in_kernel,
        grid=(nj, pl.cdiv(m, tm)),
        in_specs=[
            pl.BlockSpec((tm, D_MODEL), lambda j, i: (i, 0)),
            pl.BlockSpec((m2, D_MODEL), lambda j, i: (0, 0)),
            pl.BlockSpec((D_MODEL, tn), lambda j, i: (0, j)),
            pl.BlockSpec((D_MODEL, tn), lambda j, i: (0, j + nj)),
        ],
        out_specs=[pl.BlockSpec((tm, tn), lambda j, i: (i, j)), pl.BlockSpec((m2, tn), lambda j, i: (0, j))],
        out_shape=[jax.ShapeDtypeStruct((m, D_FF), BF16), jax.ShapeDtypeStruct((m2, D_FF), BF16)],
        scratch_shapes=[pltpu.VMEM((D_MODEL, tn), BF16), pltpu.VMEM((D_MODEL, tn), BF16)],
        compiler_params=_params("arbitrary", "arbitrary"),
        name="ffn_in",
    )(hn, hn2, w, w)


def _ffn_out_kernel(h_ref, w_ref, x_ref, g_ref, o_ref):
    k = pl.program_id(1)

    @pl.when(k == 0)
    def _():
        o_ref[...] = x_ref[...]

    o_ref[...] += jnp.dot(h_ref[...], w_ref[...], preferred_element_type=F32)

    @pl.when(k == pl.num_programs(1) - 1)
    def _():
        x2 = o_ref[...]
        y = x2 * lax.rsqrt(jnp.mean(x2 * x2, axis=-1, keepdims=True) + EPS)
        o_ref[...] = y * g_ref[...]


def _ffn_out(h, w, x1, g, tm=1024, tk=512):
    m = h.shape[0]
    tm = min(tm, m)
    return pl.pallas_call(
        _ffn_out_kernel,
        grid=(pl.cdiv(m, tm), D_FF // tk),
        in_specs=[
            pl.BlockSpec((tm, tk), lambda i, k: (i, k)),
            pl.BlockSpec((tk, D_MODEL), lambda i, k: (k, 0)),
            pl.BlockSpec((tm, D_MODEL), lambda i, k: (i, 0)),
            pl.BlockSpec((1, D_MODEL), lambda i, k: (0, 0)),
        ],
        out_specs=pl.BlockSpec((tm, D_MODEL), lambda i, k: (i, 0)),
        out_shape=jax.ShapeDtypeStruct((m, D_MODEL), F32),
        compiler_params=_params("parallel", "arbitrary"),
        name="ffn_out",
    )(h, w, x1, g.reshape(1, D_MODEL))


def _kmean_kernel(k_ref, o_ref):
    t = k_ref.shape[0]
    nb = t // MOBA_BLOCK
    km = jnp.mean(k_ref[...].reshape(nb, MOBA_BLOCK, KV_WIDTH), axis=1)
    o_ref[0] = jnp.concatenate([km, jnp.zeros((LANES - nb, KV_WIDTH), F32)], axis=0)


def _kmean(k, batch):
    t = k.shape[0] // batch
    return pl.pallas_call(
        _kmean_kernel,
        grid=(batch,),
        in_specs=[pl.BlockSpec((t, KV_WIDTH), lambda b: (b, 0))],
        out_specs=pl.BlockSpec((1, LANES, KV_WIDTH), lambda b: (b, 0, 0)),
        out_shape=jax.ShapeDtypeStruct((batch, LANES, KV_WIDTH), F32),
        compiler_params=_params("parallel"),
        name="kmean",
    )(k)


def _top_rows(gate, row_id, n_valid, n_pick):
    picks = []
    for t in range(n_pick):
        mx = jnp.max(gate, axis=0, keepdims=True)
        idx = jnp.min(jnp.where(gate == mx, row_id, jnp.iinfo(jnp.int32).max), axis=0, keepdims=True)
        picks.append(jnp.where(t < n_valid, idx, -1))
        gate = jnp.where(row_id == idx, -jnp.inf, gate)
    return picks


V_ROWS = HEAD_DIM + 16
LOG2E = 1.4426950408889634


def _moba_kernel(slopes_ref, q_ref, k_ref, v_ref, km_ref, o_ref,
                 kb_ref, vt_ref, bias_ref, ua_ref, ub_ref, m_ref, acc_ref):
    kvh = pl.program_id(1)
    i = pl.program_id(2)
    nq = KV_GROUP * MOBA_BLOCK
    nb = vt_ref.shape[0]
    nb_rows = -(-nb // 8) * 8

    @pl.when(i == 0)
    def _():
        kb_ref[...] = k_ref[...].astype(BF16)
        for n in range(nb):
            vt_ref[n, 0:HEAD_DIM, :] = v_ref[n * MOBA_BLOCK:(n + 1) * MOBA_BLOCK, :].T.astype(BF16)
            vt_ref[n, HEAD_DIM:V_ROWS, :] = jnp.ones((V_ROWS - HEAD_DIM, MOBA_BLOCK), BF16)

    q = q_ref[...]
    qs = jnp.concatenate([q[:, g * HEAD_DIM:(g + 1) * HEAD_DIM] for g in range(KV_GROUP)], axis=0)
    qsb = (qs * (HEAD_DIM ** -0.5 * LOG2E)).astype(BF16)

    contract_last = (((1,), (1,)), ((), ()))
    q_hi, q_lo = _split_hi_lo(qs)
    km_hi, km_lo = _split_hi_lo(km_ref[0, 0:nb_rows, :])
    gate = (lax.dot_general(km_hi, q_hi, contract_last, preferred_element_type=F32)
            + lax.dot_general(km_hi, q_lo, contract_last, preferred_element_type=F32)
            + lax.dot_general(km_lo, q_hi, contract_last, preferred_element_type=F32))
    blk = lax.broadcasted_iota(jnp.int32, (nb_rows, nq), 0)
    picks = _top_rows(jnp.where(blk < i, gate, -jnp.inf), blk, i, MOBA_TOPK)

    qlane = lax.broadcasted_iota(jnp.int32, (1, nq), 1)
    slope = jnp.zeros((1, nq), F32)
    for g in range(KV_GROUP):
        slope = jnp.where(qlane // MOBA_BLOCK == g, slopes_ref[kvh * KV_GROUP + g] * LOG2E, slope)
    kidx = lax.broadcasted_iota(jnp.int32, (MOBA_BLOCK, nq), 0)
    bias_ref[...] = kidx.astype(F32) * slope

    m_ref[...] = jnp.full((1, nq), NEG_BIG, F32)
    acc_ref[...] = jnp.zeros((V_ROWS, nq), F32)

    def scores(n):
        start = pl.multiple_of(n * MOBA_BLOCK, MOBA_BLOCK)
        kb = kb_ref[pl.ds(start, MOBA_BLOCK), :]
        return lax.dot_general(kb, qsb, contract_last, preferred_element_type=F32) + bias_ref[...]

    def past_update(u_ref, n):
        u = u_ref[...]
        c = ((i - n) * MOBA_BLOCK).astype(F32) * slope
        sel = (picks[0] == n) | (picks[1] == n) | (picks[2] == n)
        m_old = m_ref[...]
        m_new = jnp.where(sel, jnp.maximum(m_old, jnp.max(u, axis=0, keepdims=True) - c), m_old)
        p = jnp.exp2(u - (jnp.where(sel, m_new, -NEG_BIG) + c))
        alpha = jnp.exp2(m_old - m_new)
        acc_ref[...] = alpha * acc_ref[...] + jnp.dot(vt_ref[n], p.astype(BF16), preferred_element_type=F32)
        m_ref[...] = m_new

    def own_update(u_ref):
        u = jnp.where(kidx <= qlane % MOBA_BLOCK, u_ref[...], NEG_BIG)
        m_old = m_ref[...]
        m_new = jnp.maximum(m_old, jnp.max(u, axis=0, keepdims=True))
        p = jnp.exp2(u - m_new)
        alpha = jnp.exp2(m_old - m_new)
        acc = alpha * acc_ref[...] + jnp.dot(vt_ref[i], p.astype(BF16), preferred_element_type=F32)
        out = acc[0:HEAD_DIM, :] / acc[HEAD_DIM:HEAD_DIM + 1, :]
        for g in range(KV_GROUP):
            o_ref[:, g * HEAD_DIM:(g + 1) * HEAD_DIM] = (
                out[:, g * MOBA_BLOCK:(g + 1) * MOBA_BLOCK].T.astype(o_ref.dtype))

    ua_ref[...] = scores(0)

    def two_blocks(k, carry):
        n = 2 * k
        ub_ref[...] = scores(n + 1)
        past_update(ua_ref, n)
        ua_ref[...] = scores(n + 2)
        past_update(ub_ref, n + 1)
        return carry

    lax.fori_loop(0, i // 2, two_blocks, 0)

    @pl.when(i % 2 == 1)
    def _():
        ub_ref[...] = scores(i)
        past_update(ua_ref, i - 1)
        own_update(ub_ref)

    @pl.when(i % 2 == 0)
    def _():
        own_update(ua_ref)


def _moba_prompt(q, k, v, kmean, slopes, batch):
    t = q.shape[0] // batch
    nb = t // MOBA_BLOCK
    assert nb <= LANES
    nq = KV_GROUP * MOBA_BLOCK
    grid_spec = pltpu.PrefetchScalarGridSpec(
        num_scalar_prefetch=1,
        grid=(batch, N_KV_HEADS, nb),
        in_specs=[
            pl.BlockSpec((MOBA_BLOCK, KV_GROUP * HEAD_DIM), lambda b, h, i, s: (b * nb + i, h)),
            pl.BlockSpec((t, HEAD_DIM), lambda b, h, i, s: (b, h)),
            pl.BlockSpec((t, HEAD_DIM), lambda b, h, i, s: (b, h)),
            pl.BlockSpec((1, LANES, HEAD_DIM), lambda b, h, i, s: (b, 0, h)),
        ],
        out_specs=pl.BlockSpec((MOBA_BLOCK, KV_GROUP * HEAD_DIM), lambda b, h, i, s: (b * nb + i, h)),
        scratch_shapes=[
            pltpu.VMEM((t, HEAD_DIM), BF16),
            pltpu.VMEM((nb, V_ROWS, MOBA_BLOCK), BF16),
            pltpu.VMEM((MOBA_BLOCK, nq), F32),
            pltpu.VMEM((MOBA_BLOCK, nq), F32),
            pltpu.VMEM((MOBA_BLOCK, nq), F32),
            pltpu.VMEM((1, nq), F32),
            pltpu.VMEM((V_ROWS, nq), F32),
        ],
    )
    return pl.pallas_call(
        _moba_kernel,
        grid_spec=grid_spec,
        out_shape=jax.ShapeDtypeStruct((batch * t, ATTN_WIDTH), BF16),
        compiler_params=_params("arbitrary", "arbitrary", "arbitrary"),
        name="moba_prompt",
    )(slopes, q, k, v, kmean)


def _ssd_prompt_kernel(xbc_ref, z_ref, dt_ref, cw_ref, cb_ref, dtb_ref, alog_ref, dsk_ref, nrm_ref, e_ref,
                       y_ref, st_ref, xpad_ref, act_ref):
    c = pl.program_id(1)
    L = SSD_CHUNK

    @pl.when(c == 0)
    def _():
        st_ref[...] = jnp.zeros_like(st_ref)
        xpad_ref[0:L, :] = jnp.zeros((L, CONV_DIM), F32)

    xpad_ref[L:2 * L, :] = xbc_ref[...]
    taps = CONV_WIDTH - 1
    out_row = lax.broadcasted_iota(jnp.int32, (taps * L, 2 * L), 0)
    in_row = lax.broadcasted_iota(jnp.int32, (taps * L, 2 * L), 1)
    shift = (in_row == L + out_row % L - (out_row // L + 1)).astype(BF16)
    width = 1024
    for j in range(CONV_DIM // width):
        cs = slice(j * width, (j + 1) * width)
        shifted = jnp.dot(shift, xpad_ref[:, cs].astype(BF16), preferred_element_type=F32)
        conv = cb_ref[:, cs] + shifted[(taps - 1) * L:taps * L] * cw_ref[0:1, cs]
        for i in range(1, taps):
            conv = conv + shifted[(taps - 1 - i) * L:(taps - i) * L] * cw_ref[i:i + 1, cs]
        conv = conv + xpad_ref[L:2 * L, cs] * cw_ref[taps:taps + 1, cs]
        act_ref[:, cs] = _silu(conv)
    xpad_ref[L - SUBLANES:L, :] = xpad_ref[2 * L - SUBLANES:2 * L, :]
    act = act_ref[...]
    xs = act[:, :D_INNER]
    bm = act[:, D_INNER:D_INNER + SSD_GROUPS * D_STATE]
    cm = act[:, D_INNER + SSD_GROUPS * D_STATE:]

    dt = _softplus(dt_ref[...] + dtb_ref[...])
    la = dt * (-jnp.exp(alog_ref[...]))
    r_i = lax.broadcasted_iota(jnp.int32, (L, L), 0)
    c_i = lax.broadcasted_iota(jnp.int32, (L, L), 1)
    causal = r_i >= c_i
    tri = causal.astype(BF16)
    la_hi = la.astype(BF16)
    la_r = la - la_hi.astype(F32)
    la_mid = la_r.astype(BF16)
    la_lo = (la_r - la_mid.astype(F32)).astype(BF16)
    acs = (jnp.dot(tri, la_hi, preferred_element_type=F32) + jnp.dot(tri, la_mid, preferred_element_type=F32)
           + jnp.dot(tri, la_lo, preferred_element_type=F32))
    acs_t = acs.T
    dt_t = dt.T
    acs_last = acs[L - 1:L, :]
    e = e_ref[...]
    exp_acs_x = jnp.dot(jnp.exp(acs).astype(BF16), e, preferred_element_type=F32)
    wend_x = jnp.dot((dt * jnp.exp(acs_last - acs)).astype(BF16), e, preferred_element_type=F32)
    dec_last = jnp.exp(acs_t[:, L - 1:L])
    lane = lax.broadcasted_iota(jnp.int32, (L, LANES), 1)
    contract_last = (((1,), (1,)), ((), ()))
    contract_first = (((0,), (0,)), ((), ()))
    z = z_ref[...]

    for g in range(SSD_GROUPS):
        gs = slice(g * GROUP_WIDTH, (g + 1) * GROUP_WIDTH)
        bg = bm[:, g * D_STATE:(g + 1) * D_STATE].astype(BF16)
        cg = cm[:, g * D_STATE:(g + 1) * D_STATE].astype(BF16)
        cb = lax.dot_general(cg, bg, contract_last, preferred_element_type=F32)
        st = st_ref[0, gs, :]
        y_state = lax.dot_general(cg, st.astype(BF16), contract_last, preferred_element_type=F32)
        pairs = []
        for j in range(GROUP_WIDTH // LANES):
            col = g * (GROUP_WIDTH // LANES) + j
            ms = []
            for h in (2 * col, 2 * col + 1):
                seg = acs[:, h:h + 1] - acs_t[h:h + 1, :]
                ms.append(cb * jnp.exp(jnp.where(causal, seg, -jnp.inf)) * dt_t[h:h + 1, :])
            xcol = xs[:, col * LANES:(col + 1) * LANES]
            rhs = jnp.concatenate([jnp.where(lane < SSD_HEAD_DIM, xcol, 0.0),
                                   jnp.where(lane >= SSD_HEAD_DIM, xcol, 0.0)], axis=0).astype(BF16)
            lhs = jnp.concatenate(ms, axis=1).astype(BF16)
            pairs.append(jnp.dot(lhs, rhs, preferred_element_type=F32))
        xg = xs[:, gs]
        yg = jnp.concatenate(pairs, axis=1) + y_state * exp_acs_x[:, gs] + dsk_ref[:, gs] * xg
        yg = yg * _silu(z[:, gs])
        yg = yg * lax.rsqrt(jnp.mean(yg * yg, axis=-1, keepdims=True) + EPS)
        y_ref[:, gs] = (yg * nrm_ref[:, gs]).astype(y_ref.dtype)

        upd = lax.dot_general((xg * wend_x[:, gs]).astype(BF16), bg, contract_first, preferred_element_type=F32)
        drows = jnp.concatenate(
            [jnp.broadcast_to(dec_last[h:h + 1, :], (SSD_HEAD_DIM, D_STATE))
             for h in range(g * 8, (g + 1) * 8)], axis=0)
        st_ref[0, gs, :] = st * drows + upd


def _ssd_prompt(xbc, z, dt_raw, cw, cb, dtb, alog, dsk_x, nrm, e, batch):
    t = xbc.shape[0] // batch
    nc = t // SSD_CHUNK
    row = lambda b, c: (b * nc + c, 0)
    const = lambda b, c: (0, 0)
    return pl.pallas_call(
        _ssd_prompt_kernel,
        grid=(batch, nc),
        in_specs=[
            pl.BlockSpec((SSD_CHUNK, CONV_DIM), row),
            pl.BlockSpec((SSD_CHUNK, D_INNER), row),
            pl.BlockSpec((SSD_CHUNK, LANES), row),
            pl.BlockSpec((CONV_WIDTH, CONV_DIM), const),
            pl.BlockSpec((1, CONV_DIM), const),
            pl.BlockSpec((1, LANES), const),
            pl.BlockSpec((1, LANES), const),
            pl.BlockSpec((1, D_INNER), const),
            pl.BlockSpec((1, D_INNER), const),
            pl.BlockSpec((LANES, D_INNER), const),
        ],
        out_specs=[
            pl.BlockSpec((SSD_CHUNK, D_INNER), row),
            pl.BlockSpec((1, D_INNER, D_STATE), lambda b, c: (b, 0, 0)),
        ],
        out_shape=[
            jax.ShapeDtypeStruct((batch * t, D_INNER), BF16),
            jax.ShapeDtypeStruct((batch, D_INNER, D_STATE), F32),
        ],
        scratch_shapes=[pltpu.VMEM((2 * SSD_CHUNK, CONV_DIM), F32), pltpu.VMEM((SSD_CHUNK, CONV_DIM), F32)],
        compiler_params=_params("arbitrary", "arbitrary"),
        name="ssd_prompt",
    )(xbc, z, dt_raw, cw, cb, dtb, alog, dsk_x, nrm, e)


KMEAN_PAGES_PER_STEP = 16


def _dec_kmean_kernel(pt_ref, *refs):
    k_refs, o_ref = refs[:-1], refs[-1]
    rows = k_refs[0].shape[1]
    pages_per_block = MOBA_BLOCK * N_KV_HEADS // rows
    for j in range(len(k_refs) // pages_per_block):
        acc = jnp.zeros((SUBLANES, HEAD_DIM), F32)
        for page in k_refs[j * pages_per_block:(j + 1) * pages_per_block]:
            acc = acc + jnp.sum(page[0].reshape(rows // SUBLANES, SUBLANES, HEAD_DIM), axis=0)
        km = (acc[0:N_KV_HEADS] + acc[N_KV_HEADS:SUBLANES]) * (1.0 / MOBA_BLOCK)
        for c in range(N_KV_HEADS):
            o_ref[0, c, j:j + 1, :] = km[c:c + 1, :]


def _dec_kmean(kc, page_table, nb):
    nseq, n_pages = page_table.shape
    pps = min(KMEAN_PAGES_PER_STEP, n_pages)
    rows = kc.shape[1]
    bps = pps * rows // (MOBA_BLOCK * N_KV_HEADS)
    page_spec = lambda j: pl.BlockSpec((1, rows, HEAD_DIM), lambda b, s, pt: (pt[b, s * pps + j], 0, 0))
    grid_spec = pltpu.PrefetchScalarGridSpec(
        num_scalar_prefetch=1,
        grid=(nseq, n_pages // pps),
        in_specs=[page_spec(j) for j in range(pps)],
        out_specs=pl.BlockSpec((1, N_KV_HEADS, bps, HEAD_DIM), lambda b, s, pt: (b, 0, s, 0)),
    )
    return pl.pallas_call(
        _dec_kmean_kernel,
        grid_spec=grid_spec,
        out_shape=jax.ShapeDtypeStruct((nseq, N_KV_HEADS, nb, HEAD_DIM), F32),
        compiler_params=_params("arbitrary", "arbitrary"),
        name="dec_kmean",
    )(page_table, *([kc] * pps))


def _dec_select_kernel(q_ref, km_ref, sel_ref):
    nb = km_ref.shape[2]
    hrow = lax.broadcasted_iota(jnp.int32, (N_HEADS, 1), 0)
    contract_last = (((1,), (1,)), ((), ()))
    q_hi, q_lo = _split_hi_lo(q_ref[0])
    gate = jnp.zeros((N_HEADS, nb), F32)
    for c in range(N_KV_HEADS):
        km_hi, km_lo = _split_hi_lo(km_ref[0, c])
        g = (lax.dot_general(q_hi, km_hi, contract_last, preferred_element_type=F32)
             + lax.dot_general(q_lo, km_hi, contract_last, preferred_element_type=F32)
             + lax.dot_general(q_hi, km_lo, contract_last, preferred_element_type=F32))
        gate = jnp.where(hrow // KV_GROUP == c, g, gate)
    blk = lax.broadcasted_iota(jnp.int32, (N_HEADS, nb), 1)
    lane = lax.broadcasted_iota(jnp.int32, (N_HEADS, LANES), 1)
    sel = jnp.zeros((N_HEADS, LANES), jnp.int32)
    for t in range(MOBA_TOPK):
        mx = jnp.max(gate, axis=-1, keepdims=True)
        idx = jnp.min(jnp.where(gate == mx, blk, nb), axis=-1, keepdims=True)
        sel = jnp.where(lane == t, idx, sel)
        gate = jnp.where(blk == idx, -jnp.inf, gate)
    sel_ref[0] = sel


def _dec_select(q, kmean):
    nseq, _, nb, _ = kmean.shape
    assert nb >= MOBA_TOPK
    return pl.pallas_call(
        _dec_select_kernel,
        grid=(nseq,),
        in_specs=[pl.BlockSpec((1, N_HEADS, HEAD_DIM), lambda b: (b, 0, 0)),
                  pl.BlockSpec((1, N_KV_HEADS, nb, HEAD_DIM), lambda b: (b, 0, 0, 0))],
        out_specs=pl.BlockSpec((1, N_HEADS, LANES), lambda b: (b, 0, 0)),
        out_shape=jax.ShapeDtypeStruct((nseq, N_HEADS, LANES), jnp.int32),
        compiler_params=_params("parallel"),
        name="dec_select",
    )(q.reshape(nseq, N_HEADS, HEAD_DIM), kmean)


def _dec_attn_kernel(pt_ref, sel_ref, slopes_ref, q_ref, kn_ref, vn_ref, kc_ref, vc_ref, o_ref,
                     kbuf, vbuf, sem, *, past_len, pages_per_block):
    n_sel = MOBA_TOPK * pages_per_block
    b = pl.program_id(0)
    h = pl.program_id(1)
    step = b * N_HEADS + h
    n_steps = pl.num_programs(0) * N_HEADS
    slot = step % 2
    keys_per_page = kbuf.shape[2]

    def page_copies(at_step, at_slot):
        bb = at_step // N_HEADS
        hh = at_step % N_HEADS
        cc = hh // KV_GROUP
        copies = []
        for t in range(MOBA_TOPK):
            blk = sel_ref[bb, hh * MOBA_TOPK + t]
            for j in range(pages_per_block):
                page = pt_ref[bb, pages_per_block * blk + j]
                i = t * pages_per_block + j
                copies.append(pltpu.make_async_copy(kc_ref.at[page, :, cc, :], kbuf.at[at_slot, i], sem.at[at_slot, i]))
                copies.append(pltpu.make_async_copy(vc_ref.at[page, :, cc, :], vbuf.at[at_slot, i],
                                                    sem.at[at_slot, n_sel + i]))
        return copies

    @pl.when(step == 0)
    def _():
        for copy in page_copies(step, slot):
            copy.start()

    @pl.when(step + 1 < n_steps)
    def _():
        for copy in page_copies(step + 1, 1 - slot):
            copy.start()

    for copy in page_copies(step, slot):
        copy.wait()

    c = h // KV_GROUP
    scale = HEAD_DIM ** -0.5
    qh = q_ref[0, pl.ds(h, 1), :]
    slope = slopes_ref[h]
    r = lax.broadcasted_iota(jnp.int32, (keys_per_page, 1), 0)
    scores = []
    for t in range(MOBA_TOPK):
        blk = sel_ref[b, h * MOBA_TOPK + t]
        for j in range(pages_per_block):
            kp = kbuf[slot, t * pages_per_block + j]
            s = jnp.sum(kp * qh, axis=-1, keepdims=True) * scale
            kpos = blk * MOBA_BLOCK + j * keys_per_page + r
            scores.append(s - slope * (past_len - kpos).astype(F32))
    kn = kn_ref[0, pl.ds(c, 1), :]
    vn = vn_ref[0, pl.ds(c, 1), :]
    s_self = jnp.sum(qh * kn, axis=-1, keepdims=True) * scale
    m = s_self
    for s in scores:
        m = jnp.maximum(m, jnp.max(s, axis=0, keepdims=True))
    w_self = jnp.exp(s_self - m)
    l = w_self
    o = w_self * vn
    for i, s in enumerate(scores):
        p = jnp.exp(s - m)
        l = l + jnp.sum(p, axis=0, keepdims=True)
        o = o + jnp.sum(p * vbuf[slot, i], axis=0, keepdims=True)
    o_ref[0, 0] = o / l


def _dec_attn(q, k_new, v_new, kc, vc, page_table, sel, slopes):
    nseq, n_pages = page_table.shape
    keys_per_page = kc.shape[1]
    ppb = MOBA_BLOCK // keys_per_page
    n_sel = MOBA_TOPK * ppb
    kv_spec = pl.BlockSpec((1, N_KV_HEADS, HEAD_DIM), lambda b, h, pt, sel, sl: (b, 0, 0))
    grid_spec = pltpu.PrefetchScalarGridSpec(
        num_scalar_prefetch=3,
        grid=(nseq, N_HEADS),
        in_specs=[pl.BlockSpec((1, N_HEADS, HEAD_DIM), lambda b, h, pt, sel, sl: (b, 0, 0)), kv_spec, kv_spec,
                  pl.BlockSpec(memory_space=pl.ANY), pl.BlockSpec(memory_space=pl.ANY)],
        out_specs=pl.BlockSpec((1, 1, 1, HEAD_DIM), lambda b, h, pt, sel, sl: (b, h, 0, 0)),
        scratch_shapes=[pltpu.VMEM((2, n_sel, keys_per_page, HEAD_DIM), F32),
                        pltpu.VMEM((2, n_sel, keys_per_page, HEAD_DIM), F32),
                        pltpu.SemaphoreType.DMA((2, 2 * n_sel))],
    )
    out = pl.pallas_call(
        functools.partial(_dec_attn_kernel, past_len=n_pages * keys_per_page, pages_per_block=ppb),
        grid_spec=grid_spec,
        out_shape=jax.ShapeDtypeStruct((nseq, N_HEADS, 1, HEAD_DIM), F32),
        compiler_params=_params("arbitrary", "arbitrary"),
        name="dec_attn",
    )(page_table, sel, slopes, q.reshape(nseq, N_HEADS, HEAD_DIM), k_new.reshape(nseq, N_KV_HEADS, HEAD_DIM),
      v_new.reshape(nseq, N_KV_HEADS, HEAD_DIM), kc, vc)
    return out.reshape(nseq, ATTN_WIDTH).astype(BF16)


def _ssd_step_prep_kernel(xbc_ref, cs_ref, dt_ref, cw_ref, cb_ref, dtb_ref, alog_ref, e_ref,
                          xs_ref, b_ref, c_ref, xd_ref, da_ref, cn_ref):
    x = xbc_ref[...]
    conv = cb_ref[...] + cs_ref[0] * cw_ref[0:1, :]
    for i in range(1, CONV_WIDTH - 1):
        conv = conv + cs_ref[i] * cw_ref[i:i + 1, :]
    conv = conv + x * cw_ref[CONV_WIDTH - 1:CONV_WIDTH, :]
    for i in range(CONV_WIDTH - 2):
        cn_ref[i] = cs_ref[i + 1]
    cn_ref[CONV_WIDTH - 2] = x
    act = _silu(conv)
    xs = act[:, :D_INNER]
    xs_ref[...] = xs
    b_ref[...] = act[:, D_INNER:D_INNER + SSD_GROUPS * D_STATE]
    c_ref[...] = act[:, D_INNER + SSD_GROUPS * D_STATE:]
    dt = _softplus(dt_ref[...] + dtb_ref[...])
    da = jnp.exp(dt * (-jnp.exp(alog_ref[...])))
    e = e_ref[...]
    dt_hi, dt_lo = _split_hi_lo(dt)
    da_hi, da_lo = _split_hi_lo(da)
    dt_x = jnp.dot(dt_hi, e, preferred_element_type=F32) + jnp.dot(dt_lo, e, preferred_element_type=F32)
    da_ref[...] = jnp.dot(da_hi, e, preferred_element_type=F32) + jnp.dot(da_lo, e, preferred_element_type=F32)
    xd_ref[...] = xs * dt_x


def _ssd_step_prep(xbc, conv_state_t, dt_raw, cw, cb, dtb, alog, e):
    n = xbc.shape[0]
    sds = lambda *shape: jax.ShapeDtypeStruct(shape, F32)
    return pl.pallas_call(
        _ssd_step_prep_kernel,
        out_shape=[sds(n, D_INNER), sds(n, SSD_GROUPS * D_STATE), sds(n, SSD_GROUPS * D_STATE),
                   sds(n, D_INNER), sds(n, D_INNER), sds(CONV_WIDTH - 1, n, CONV_DIM)],
        compiler_params=pltpu.CompilerParams(vmem_limit_bytes=VMEM_LIMIT),
        name="ssd_step_prep",
    )(xbc, conv_state_t, dt_raw, cw, cb, dtb, alog, e)


def _ssd_step_kernel(xd_ref, da_ref, xs_ref, z_ref, b_ref, c_ref, st_ref, dsk_ref, nrm_ref, y_ref, sn_ref):
    sub = 8
    row0 = lax.broadcasted_iota(jnp.int32, (sub, D_STATE), 0) == 0
    contract_last = (((1,), (1,)), ((), ()))
    contract_first = (((0,), (0,)), ((), ()))
    ones0 = jnp.where(row0, 1.0, 0.0).astype(BF16)
    for g in range(SSD_GROUPS):
        gs = slice(g * GROUP_WIDTH, (g + 1) * GROUP_WIDTH)
        ns = slice(g * D_STATE, (g + 1) * D_STATE)
        xd8 = jnp.broadcast_to(xd_ref[0, :, gs], (sub, GROUP_WIDTH))
        da8 = jnp.broadcast_to(da_ref[0, :, gs], (sub, GROUP_WIDTH))
        b8 = jnp.where(row0, jnp.broadcast_to(b_ref[0, :, ns], (sub, D_STATE)), 0.0)
        c8 = jnp.broadcast_to(c_ref[0, :, ns], (sub, D_STATE)).astype(BF16)
        xd_hi, xd_lo = _split_hi_lo(xd8)
        b_hi, b_lo = _split_hi_lo(b8)
        outer = lambda a, b: lax.dot_general(a, b, contract_first, preferred_element_type=F32)
        upd = outer(xd_hi, b_hi) + outer(xd_lo, b_hi) + outer(xd_hi, b_lo)
        da_hi = da8.astype(BF16)
        da_r = da8 - da_hi.astype(F32)
        da_mid = da_r.astype(BF16)
        da_lo = (da_r - da_mid.astype(F32)).astype(BF16)
        dec = outer(da_hi, ones0) + outer(da_mid, ones0) + outer(da_lo, ones0)
        st_new = st_ref[0, gs, :] * dec + upd
        sn_ref[0, gs, :] = st_new
        yg = lax.dot_general(c8, st_new.astype(BF16), contract_last, preferred_element_type=F32)[0:1, :]
        yg = yg + dsk_ref[:, gs] * xs_ref[0, :, gs]
        yg = yg * _silu(z_ref[0, :, gs])
        yg = yg * lax.rsqrt(jnp.mean(yg * yg, axis=-1, keepdims=True) + EPS)
        y_ref[0, :, gs] = (yg * nrm_ref[:, gs]).astype(y_ref.dtype)


def _ssd_step(xd, da, xs, z, bv, cv, state, dsk_x, nrm):
    n = xd.shape[0]
    r3 = lambda a: a.reshape(n, 1, a.shape[-1])
    wide = pl.BlockSpec((1, 1, D_INNER), lambda b: (b, 0, 0))
    narrow = pl.BlockSpec((1, 1, SSD_GROUPS * D_STATE), lambda b: (b, 0, 0))
    st_spec = pl.BlockSpec((1, D_INNER, D_STATE), lambda b: (b, 0, 0))
    const = pl.BlockSpec((1, D_INNER), lambda b: (0, 0))
    y, st_new = pl.pallas_call(
        _ssd_step_kernel,
        grid=(n,),
        in_specs=[wide, wide, wide, wide, narrow, narrow, st_spec, const, const],
        out_specs=[wide, st_spec],
        out_shape=[jax.ShapeDtypeStruct((n, 1, D_INNER), BF16), jax.ShapeDtypeStruct((n, D_INNER, D_STATE), F32)],
        compiler_params=_params("parallel"),
        name="ssd_step",
    )(r3(xd), r3(da), r3(xs), r3(z), r3(bv), r3(cv), state, dsk_x, nrm)
    return y.reshape(n, D_INNER), st_new


def _alibi_slopes():
    return jnp.exp2(-8.0 * jnp.arange(1, N_HEADS + 1, dtype=F32) / N_HEADS)


def _prep_weights(norm_mix, w_in, conv_w, conv_b, dt_bias, a_log, d_skip, ssd_norm, w_attn_out, w_ssd_out, w_o,
                  norm_ffn, w_ffn_in, w_ffn_out):
    pad_heads = lambda a: jnp.pad(a.astype(F32), (0, LANES - SSD_HEADS)).reshape(1, LANES)
    head_of_channel = jnp.arange(D_INNER, dtype=jnp.int32) // SSD_HEAD_DIM
    expand = (jnp.arange(LANES, dtype=jnp.int32)[:, None] == head_of_channel[None, :]).astype(BF16)
    return dict(
        norm_mix=norm_mix, w_in_t=w_in.T, conv_w=conv_w, conv_b=conv_b.reshape(1, CONV_DIM),
        dt_bias=pad_heads(dt_bias), a_log=pad_heads(a_log),
        d_skip_x=jnp.repeat(d_skip.astype(F32), SSD_HEAD_DIM).reshape(1, D_INNER),
        ssd_norm=ssd_norm.astype(F32).reshape(1, D_INNER), expand=expand,
        w_attn_out=w_attn_out.astype(BF16), w_ssd_out=w_ssd_out.astype(BF16), w_o=w_o.astype(BF16),
        norm_ffn=norm_ffn, w_ffn_in=w_ffn_in, w_ffn_out=w_ffn_out.astype(BF16),
    )


W_IN_SEGMENTS = (("gates", 2 * D_MODEL), ("q", ATTN_WIDTH), ("k", KV_WIDTH), ("v", KV_WIDTH), ("z", D_INNER),
                 ("xbc", CONV_DIM))
W_IN_MAIN = sum(size for _, size in W_IN_SEGMENTS)


def _project(x, x2, w):
    hn, hn2 = _rmsnorm(x, w["norm_mix"]), _rmsnorm(x2, w["norm_mix"])
    out, out2, col0 = {}, {}, 0
    for name, size in W_IN_SEGMENTS:
        out[name], out2[name] = _matmul_wt32(hn, hn2, w["w_in_t"], col0, size, name="proj_" + name)
        col0 += size
    out["dt"], out2["dt"] = _matmul_wt32_tail(hn, hn2, w["w_in_t"], col0, name="proj_dt")
    return out, out2


def _mix_oproj(x, att, y_ssd, gates, w):
    mixed = _mix(att, y_ssd, w["w_attn_out"], w["w_ssd_out"], gates)
    return _oproj(mixed, w["w_o"], x, w["norm_ffn"])


def _prompt_mixers(p, w, slopes, batch):
    t = p["q"].shape[0] // batch
    kmean = _kmean(p["k"], batch)
    att = _moba_prompt(p["q"], p["k"], p["v"], kmean, slopes, batch)
    y_ssd, state = _ssd_prompt(p["xbc"], p["z"], p["dt"], w["conv_w"], w["conv_b"], w["dt_bias"], w["a_log"],
                               w["d_skip_x"], w["ssd_norm"], w["expand"], batch)
    conv_new = p["xbc"].reshape(batch, t, CONV_DIM)[:, t - (CONV_WIDTH - 1):, :]
    return att, y_ssd, conv_new, state


def _decode_mixers(p, cache_k, cache_v, page_table, conv_state, ssm_state, w, slopes):
    n = p["q"].shape[0]
    n_phys, page = cache_k.shape[0], cache_k.shape[1]
    kc = cache_k.reshape(n_phys, page * N_KV_HEADS, HEAD_DIM)
    nb = page_table.shape[1] * page // MOBA_BLOCK
    sel = _dec_select(p["q"], _dec_kmean(kc, page_table, nb))
    att = _dec_attn(p["q"], p["k"], p["v"], cache_k, cache_v, page_table,
                    sel[:, :, :MOBA_TOPK].reshape(n, N_HEADS * MOBA_TOPK), slopes)
    xs, bv, cv, xd, da, conv_new_t = _ssd_step_prep(
        p["xbc"], jnp.transpose(conv_state, (1, 0, 2)), p["dt"], w["conv_w"], w["conv_b"], w["dt_bias"],
        w["a_log"], w["expand"])
    y_ssd, state = _ssd_step(xd, da, xs, p["z"], bv, cv, ssm_state.reshape(n, D_INNER, D_STATE),
                             w["d_skip_x"], w["ssd_norm"])
    return att, y_ssd, jnp.transpose(conv_new_t, (1, 0, 2)), state


def kernel(x_prompt, x_sample, cache_k, cache_v, state_conv, state_ssm, page_table, norm_mix, w_in, conv_w, conv_b,
           dt_bias, a_log, d_skip, ssd_norm, w_attn_out, w_ssd_out, w_o, norm_ffn, w_ffn_in, w_ffn_out, norm_final):
    assert w_in.shape[0] == 1, "single-layer kernel"
    bp, t, _ = x_prompt.shape
    nd = x_sample.shape[0]
    assert x_sample.shape[1] == 1
    slopes = _alibi_slopes()
    w = _prep_weights(norm_mix[0], w_in[0], conv_w[0], conv_b[0], dt_bias[0], a_log[0], d_skip[0], ssd_norm[0],
                      w_attn_out[0], w_ssd_out[0], w_o[0], norm_ffn[0], w_ffn_in[0], w_ffn_out[0])

    xp, xs = x_prompt.reshape(bp * t, D_MODEL), x_sample.reshape(nd, D_MODEL)
    pp, ps = _project(xp, xs, w)
    att_p, ssd_p, cp, hp = _prompt_mixers(pp, w, slopes, bp)
    att_s, ssd_s, cs, hs = _decode_mixers(ps, cache_k[0], cache_v[0], page_table, state_conv[0], state_ssm[0],
                                          w, slopes)
    x1p, hnp = _mix_oproj(xp, att_p, ssd_p, pp["gates"], w)
    x1s, hns = _mix_oproj(xs, att_s, ssd_s, ps["gates"], w)
    ffp, ffs = _ffn_in(hnp, hns, w["w_ffn_in"])
    yp = _ffn_out(ffp, w["w_ffn_out"], x1p, norm_final)
    ys = _ffn_out(ffs, w["w_ffn_out"], x1s, norm_final)
    kp, vp, ks, vs = pp["k"], pp["v"], ps["k"], ps["v"]
    kv = lambda a, b, s: a.reshape(1, b, s, N_KV_HEADS, HEAD_DIM)
    st = lambda a, b: a.reshape(1, b, SSD_HEADS, SSD_HEAD_DIM, D_STATE)
    return (yp.reshape(bp, t, D_MODEL), ys.reshape(nd, 1, D_MODEL),
            kv(kp, bp, t), kv(vp, bp, t), cp[None], st(hp, bp),
            kv(ks, nd, 1), kv(vs, nd, 1), cs[None], st(hs, nd))
```
